```python
import math
import jax, jax.numpy as jnp
from jax import lax
import numpy as np

D_MODEL = 2048
BATCH = 4
SEQ = 2048
DEPTH = 2
DEC_BATCH = 128
DEC_SEQ = 8
PAST_LEN = 16384
PAGE_SIZE = 128

N_MIXERS = 2
N_RET_LAYERS = (DEPTH + 1) // 2
N_GDN_LAYERS = DEPTH // 2

RET_HEADS = D_MODEL // 256
RET_DK = 256
RET_DV = 512
RET_CHUNK = 64
RET_IN = 2 * RET_HEADS * RET_DK + 2 * RET_HEADS * RET_DV
ROPE_BASE = 10000.0

GDN_K_HEADS = D_MODEL // 128
GDN_V_HEADS = 2 * GDN_K_HEADS
GDN_DK = 128
GDN_DV = 128
GDN_CHUNK = 64
CONV_W = 4
CONV_DIM = 2 * GDN_K_HEADS * GDN_DK + GDN_V_HEADS * GDN_DV
GDN_IN = CONV_DIM + GDN_V_HEADS * GDN_DV + 2 * GDN_V_HEADS

MOE_GROUPS = 8
EXPERTS_PER_GROUP = 8
N_EXPERTS = MOE_GROUPS * EXPERTS_PER_GROUP
TOP_K_IN_GROUP = 2
D_EXPERT = D_MODEL // 4
MOE_BLOCK = 128

EPS = 1e-6

kernel_name = "hybrid_retention_gdn_hmoe_step"


def _rmsnorm(x, w):
    x32 = x.astype(jnp.float32)
    y = x32 * lax.rsqrt(jnp.mean(x32 * x32, axis=-1, keepdims=True) + EPS)
    return (y * w.astype(jnp.float32)).astype(x.dtype)


def _l2norm(x):
    return x * lax.rsqrt(jnp.sum(x * x, axis=-1, keepdims=True) + EPS)


def _rotary(x, pos):
    half = x.shape[-1] // 2
    inv = ROPE_BASE ** (-jnp.arange(half, dtype=jnp.float32) / half)
    ang = pos[:, None] * inv[None, :]
    cos = jnp.cos(ang)[None, :, None, :]
    sin = jnp.sin(ang)[None, :, None, :]
    x1, x2 = x[..., :half], x[..., half:]
    return jnp.concatenate([x1 * cos - x2 * sin, x1 * sin + x2 * cos], axis=-1)


def _to_chunks(x, c):
    b, l = x.shape[:2]
    x = x.reshape((b, l // c, c) + x.shape[2:])
    return jnp.moveaxis(x, (1, 2), (0, 3))


def _from_chunks(o):
    o = jnp.moveaxis(o, (0, 3), (1, 2))
    n_b, n_n, n_c, n_h, n_d = o.shape
    return o.reshape(n_b, n_n * n_c, n_h, n_d)


def _retention_chunked(q, k, v, s0, log_gamma):
    length = q.shape[1]
    c = math.gcd(length, RET_CHUNK)
    idx = jnp.arange(c, dtype=jnp.float32)
    diff = idx[:, None] - idx[None, :]
    mask = diff >= 0
    decay = jnp.where(mask, jnp.exp(log_gamma[:, None, None] * jnp.where(mask, diff, 0.0)), 0.0)
    q_decay = jnp.exp(log_gamma[:, None] * (idx + 1.0))[None, :, :, None]
    k_decay = jnp.exp(log_gamma[:, None] * (c - 1.0 - idx))[None, :, :, None]
    chunk_decay = jnp.exp(log_gamma * c)[None, :, None, None]

    def step(s, inp):
        qi, ki, vi = inp
        scores = jnp.einsum('bhid,bhjd->bhij', qi, ki) * decay[None]
        inner = jnp.einsum('bhij,bhjv->bhiv', scores, vi)
        cross = jnp.einsum('bhid,bhdv->bhiv', qi, s) * q_decay
        s_new = s * chunk_decay + jnp.einsum('bhjd,bhjv->bhdv', ki * k_decay, vi)
        return s_new, inner + cross

    s_fin, o = lax.scan(step, s0, (_to_chunks(q, c), _to_chunks(k, c), _to_chunks(v, c)))
    return _from_chunks(o), s_fin


def _retention(h, pos0, s0, w_in, gn_w, w_out):
    b, l, _ = h.shape
    hq = RET_HEADS * RET_DK
    hv = RET_HEADS * RET_DV
    proj = h @ w_in
    q, k, v, gate = jnp.split(proj, [hq, 2 * hq, 2 * hq + hv], axis=-1)
    pos = (pos0 + jnp.arange(l)).astype(jnp.float32)
    q = _rotary(q.reshape(b, l, RET_HEADS, RET_DK).astype(jnp.float32), pos)
    k = _rotary(k.reshape(b, l, RET_HEADS, RET_DK).astype(jnp.float32), pos) * (RET_DK ** -0.5)
    v = v.reshape(b, l, RET_HEADS, RET_DV).astype(jnp.float32)
    log_gamma = jnp.log1p(-jnp.exp2(-5.0 - jnp.arange(RET_HEADS, dtype=jnp.float32)))
    o, s_fin = _retention_chunked(q, k, v, s0.astype(jnp.float32), log_gamma)
    mu = jnp.mean(o, axis=-1, keepdims=True)
    var = jnp.mean(jnp.square(o - mu), axis=-1, keepdims=True)
    o = (o - mu) * lax.rsqrt(var + EPS) * gn_w.astype(jnp.float32)
    o = o.reshape(b, l, hv).astype(h.dtype) * jax.nn.silu(gate)
    return o @ w_out, s_fin


def _gated_delta_chunked(q, k, v, beta, g, s0):
    length = q.shape[1]
    c = math.gcd(length, GDN_CHUNK)
    ii = jnp.arange(c)
    tri_incl = ii[:, None] >= ii[None, :]
    tri_strict = ii[:, None] > ii[None, :]
    eye = jnp.eye(c, dtype=jnp.float32)

    def step(s, inp):
        qi, ki, vi, bi, gi = inp
        gc = jnp.cumsum(gi, axis=-1)
        decay = jnp.exp(jnp.where(tri_incl, gc[..., :, None] - gc[..., None, :], -jnp.inf))
        kb = ki * bi[..., None]
        m = jnp.where(tri_strict, jnp.einsum('bhid,bhjd->bhij', kb, ki) * decay, 0.0)
        rhs = jnp.concatenate([vi * bi[..., None], kb * jnp.exp(gc)[..., None]], axis=-1)
        sol = lax.linalg.triangular_solve(m + eye, rhs, left_side=True, lower=True)
        u, w = sol[..., :GDN_DV], sol[..., GDN_DV:]
        v_new = u - jnp.einsum('bhid,bhdv->bhiv', w, s)
        attn = jnp.einsum('bhid,bhjd->bhij', qi, ki) * decay
        o = (jnp.einsum('bhid,bhdv->bhiv', qi * jnp.exp(gc)[..., None], s)
             + jnp.einsum('bhij,bhjv->bhiv', attn, v_new))
        g_last = gc[..., -1:]
        s_new = (s * jnp.exp(g_last)[..., None]
                 + jnp.einsum('bhjd,bhjv->bhdv', ki * jnp.exp(g_last - gc)[..., None], v_new))
        return s_new, o

    xs = (_to_chunks(q, c), _to_chunks(k, c), _to_chunks(v, c), _to_chunks(beta, c), _to_chunks(g, c))
    s_fin, o = lax.scan(step, s0, xs)
    return _from_chunks(o), s_fin


def _gated_deltanet(h, conv0, s0, w_in, conv_w, a_log, dt_bias, norm_w, w_out):
    b, l, _ = h.shape
    hv = GDN_V_HEADS * GDN_DV
    proj = h @ w_in
    qkv, z, bt, a = jnp.split(proj, [CONV_DIM, CONV_DIM + hv, CONV_DIM + hv + GDN_V_HEADS], axis=-1)
    xc = jnp.concatenate([conv0.astype(qkv.dtype), qkv], axis=1)
    conv = xc[:, 0:l] * conv_w[0]
    for w in range(1, CONV_W):
        conv = conv + xc[:, w:w + l] * conv_w[w]
    new_conv = xc[:, l:]
    act = jax.nn.silu(conv).astype(jnp.float32)
    hk = GDN_K_HEADS * GDN_DK
    q, k, v = jnp.split(act, [hk, 2 * hk], axis=-1)
    rep = GDN_V_HEADS // GDN_K_HEADS
    q = jnp.repeat(_l2norm(q.reshape(b, l, GDN_K_HEADS, GDN_DK)) * (GDN_DK ** -0.5), rep, axis=2)
    k = jnp.repeat(_l2norm(k.reshape(b, l, GDN_K_HEADS, GDN_DK)), rep, axis=2)
    v = v.reshape(b, l, GDN_V_HEADS, GDN_DV)
    beta = jax.nn.sigmoid(bt.astype(jnp.float32))
    g = -jnp.exp(a_log.astype(jnp.float32)) * jax.nn.softplus(a.astype(jnp.float32) + dt_bias.astype(jnp.float32))
    o, s_fin = _gated_delta_chunked(q, k, v, beta, g, s0.astype(jnp.float32))
    o = o * lax.rsqrt(jnp.mean(o * o, axis=-1, keepdims=True) + EPS) * norm_w.astype(jnp.float32)
    o = o.reshape(b, l, hv).astype(h.dtype) * jax.nn.silu(z)
    return o @ w_out, s_fin, new_conv


def _hier_moe(h, w_group, b_group, w_expert, b_expert, w_in, w_out):
    b, l, d = h.shape
    t = h.reshape(b * l, d)
    n_tok = b * l
    g_logits = (t @ w_group).astype(jnp.float32) + b_group.astype(jnp.float32)
    g_sel = jnp.argmax(g_logits, axis=-1)
    g_w = jnp.take_along_axis(jax.nn.softmax(g_logits, axis=-1), g_sel[:, None], axis=1)
    e_logits = ((t @ w_expert).astype(jnp.float32) + b_expert.astype(jnp.float32)).reshape(n_tok, MOE_GROUPS, EXPERTS_PER_GROUP)
    e_logits = jnp.take_along_axis(e_logits, g_sel[:, None, None], axis=1)[:, 0]
    top_v, top_i = lax.top_k(e_logits, TOP_K_IN_GROUP)
    weights = g_w * jax.nn.softmax(top_v, axis=-1)
    expert_id = (g_sel[:, None] * EXPERTS_PER_GROUP + top_i).astype(jnp.int32)
    n_assign = n_tok * TOP_K_IN_GROUP
    flat_e = expert_id.reshape(n_assign)
    order = jnp.argsort(flat_e)
    sorted_e = flat_e[order]
    tok_sorted = (order // TOP_K_IN_GROUP).astype(jnp.int32)
    w_sorted = weights.reshape(n_assign)[order]
    counts = jnp.bincount(flat_e, length=N_EXPERTS)
    padded = ((counts + MOE_BLOCK - 1) // MOE_BLOCK) * MOE_BLOCK
    pad_end = jnp.cumsum(padded)
    pad_start = pad_end - padded
    start = jnp.cumsum(counts) - counts
    dest = pad_start[sorted_e] + (jnp.arange(n_assign) - start[sorted_e])
    n_blocks = -(-n_assign // MOE_BLOCK) + N_EXPERTS
    n_rows = n_blocks * MOE_BLOCK
    row_tok = jnp.full((n_rows,), n_tok, jnp.int32).at[dest].set(tok_sorted)
    row_w = jnp.zeros((n_rows,), jnp.float32).at[dest].set(w_sorted)
    block_e = jnp.minimum(jnp.sum((jnp.arange(n_blocks) * MOE_BLOCK)[:, None] >= pad_end[None, :], axis=1), N_EXPERTS - 1)
    x_pad = jnp.concatenate([t, jnp.zeros((1, d), t.dtype)], axis=0)
    xs = x_pad[row_tok].reshape(n_blocks, MOE_BLOCK, d)

    def expert_block(args):
        xb, e = args
        gate, up = jnp.split(xb @ w_in[e], 2, axis=-1)
        return (jax.nn.silu(gate) * up) @ w_out[e]

    ys = lax.map(expert_block, (xs, block_e)).reshape(n_rows, d)
    y = jnp.zeros((n_tok + 1, d), jnp.float32).at[row_tok].add(ys.astype(jnp.float32) * row_w[:, None])[:n_tok]
    return y.astype(h.dtype).reshape(b, l, d)


def _trunk(x, pos0, ret_s, gdn_s, conv_s, norm_mix, norm_ffn, norm_final,
           ret_w_in, ret_gn, ret_w_out,
           gdn_w_in, gdn_conv, gdn_a_log, gdn_dt_bias, gdn_norm, gdn_w_out,
           moe_w_group, moe_b_group, moe_w_expert, moe_b_expert, moe_w_in, moe_w_out):
    ret_new, gdn_new, conv_new = [], [], []
    for i in range(DEPTH):
        j = i // N_MIXERS
        h = _rmsnorm(x, norm_mix[i])
        if i % N_MIXERS == 0:
            out, s = _retention(h, pos0, ret_s[j], ret_w_in[j], ret_gn[j], ret_w_out[j])
            ret_new.append(s.astype(ret_s.dtype))
        else:
            out, s, c = _gated_deltanet(h, conv_s[j], gdn_s[j], gdn_w_in[j], gdn_conv[j], gdn_a_log[j],
                                        gdn_dt_bias[j], gdn_norm[j], gdn_w_out[j])
            gdn_new.append(s.astype(gdn_s.dtype))
            conv_new.append(c.astype(conv_s.dtype))
        x = x + out
        x = x + _hier_moe(_rmsnorm(x, norm_ffn[i]), moe_w_group[i], moe_b_group[i], moe_w_expert[i],
                          moe_b_expert[i], moe_w_in[i], moe_w_out[i])
    y = _rmsnorm(x, norm_final)
    return y, jnp.stack(ret_new), jnp.stack(gdn_new), jnp.stack(conv_new)


def setup_inputs(seed: int = 0) -> dict:
    key = jax.random.key(seed)
    ks = jax.random.split(key, 24)
    f32 = jnp.float32
    nrm = lambda k, shape, scale: jax.random.normal(k, shape, f32) * scale
    dt = jnp.exp(jax.random.uniform(ks[13], (N_GDN_LAYERS, GDN_V_HEADS), f32, math.log(1e-3), math.log(1e-1)))
    return {
        "x_prompt": nrm(ks[0], (BATCH, SEQ, D_MODEL), 1.0),
        "x_sample": nrm(ks[1], (DEC_BATCH, DEC_SEQ, D_MODEL), 1.0),
        "state_ret": nrm(ks[2], (N_RET_LAYERS, DEC_BATCH, RET_HEADS, RET_DK, RET_DV), 0.05),
        "state_gdn": nrm(ks[3], (N_GDN_LAYERS, DEC_BATCH, GDN_V_HEADS, GDN_DK, GDN_DV), 0.1),
        "state_conv": nrm(ks[4], (N_GDN_LAYERS, DEC_BATCH, CONV_W - 1, CONV_DIM), 1.0),
        "norm_mix": 1.0 + nrm(ks[5], (DEPTH, D_MODEL), 0.02),
        "norm_ffn": 1.0 + nrm(ks[6], (DEPTH, D_MODEL), 0.02),
        "norm_final": 1.0 + nrm(ks[7], (D_MODEL,), 0.02),
        "ret_w_in": nrm(ks[8], (N_RET_LAYERS, D_MODEL, RET_IN), D_MODEL ** -0.5),
        "ret_gn": 1.0 + nrm(ks[9], (N_RET_LAYERS, RET_HEADS, RET_DV), 0.02),
        "ret_w_out": nrm(ks[10], (N_RET_LAYERS, RET_HEADS * RET_DV, D_MODEL), (RET_HEADS * RET_DV) ** -0.5),
        "gdn_w_in": nrm(ks[11], (N_GDN_LAYERS, D_MODEL, GDN_IN), D_MODEL ** -0.5),
        "gdn_conv": nrm(ks[12], (N_GDN_LAYERS, CONV_W, CONV_DIM), CONV_W ** -0.5),
        "gdn_a_log": jnp.log(jax.random.uniform(ks[14], (N_GDN_LAYERS, GDN_V_HEADS), f32, 1.0, 16.0)),
        "gdn_dt_bias": dt + jnp.log(-jnp.expm1(-dt)),
        "gdn_norm": 1.0 + nrm(ks[15], (N_GDN_LAYERS, GDN_DV), 0.02),
        "gdn_w_out": nrm(ks[16], (N_GDN_LAYERS, GDN_V_HEADS * GDN_DV, D_MODEL), (GDN_V_HEADS * GDN_DV) ** -0.5),
        "moe_w_group": nrm(ks[17], (DEPTH, D_MODEL, MOE_GROUPS), D_MODEL ** -0.5),
        "moe_b_group": nrm(ks[18], (DEPTH, MOE_GROUPS), 0.01),
        "moe_w_expert": nrm(ks[19], (DEPTH, D_MODEL, N_EXPERTS), D_MODEL ** -0.5),
        "moe_b_expert": nrm(ks[20], (DEPTH, N_EXPERTS), 0.01),
        "moe_w_in": nrm(ks[21], (DEPTH, N_EXPERTS, D_MODEL, 2 * D_EXPERT), D_MODEL ** -0.5),
        "moe_w_out": nrm(ks[22], (DEPTH, N_EXPERTS, D_EXPERT, D_MODEL), D_EXPERT ** -0.5),
    }


def reference(x_prompt, x_sample, state_ret, state_gdn, state_conv,
              norm_mix, norm_ffn, norm_final,
              ret_w_in, ret_gn, ret_w_out,
              gdn_w_in, gdn_conv, gdn_a_log, gdn_dt_bias, gdn_norm, gdn_w_out,
              moe_w_group, moe_b_group, moe_w_expert, moe_b_expert, moe_w_in, moe_w_out):
    weights = (norm_mix, norm_ffn, norm_final, ret_w_in, ret_gn, ret_w_out,
               gdn_w_in, gdn_conv, gdn_a_log, gdn_dt_bias, gdn_norm, gdn_w_out,
               moe_w_group, moe_b_group, moe_w_expert, moe_b_expert, moe_w_in, moe_w_out)
    bp = x_prompt.shape[0]
    ret0 = jnp.zeros((N_RET_LAYERS, bp) + state_ret.shape[2:], state_ret.dtype)
    gdn0 = jnp.zeros((N_GDN_LAYERS, bp) + state_gdn.shape[2:], state_gdn.dtype)
    conv0 = jnp.zeros((N_GDN_LAYERS, bp) + state_conv.shape[2:], state_conv.dtype)
    y_prompt, ret_prompt, gdn_prompt, conv_prompt = _trunk(x_prompt, 0, ret0, gdn0, conv0, *weights)
    y_sample, ret_sample, gdn_sample, conv_sample = _trunk(x_sample, PAST_LEN, state_ret, state_gdn, state_conv, *weights)
    return (y_prompt, y_sample, ret_prompt, ret_sample, gdn_prompt, gdn_sample, conv_prompt, conv_sample)
```

```python
import functools
import math

import numpy as np
import jax
import jax.numpy as jnp
from jax import lax
from jax.experimental import pallas as pl
from jax.experimental.pallas import tpu as pltpu

F32 = jnp.float32
BF16 = jnp.bfloat16

D_MODEL = 2048
PAST_LEN = 16384
RET_HEADS = 8
RET_DK = 256
RET_DV = 512
ROPE_BASE = 10000.0
GDN_K_HEADS = 16
GDN_V_HEADS = 32
GDN_DK = 128
GDN_DV = 128
CONV_W = 4
CONV_DIM = 2 * GDN_K_HEADS * GDN_DK + GDN_V_HEADS * GDN_DV
MOE_GROUPS = 8
EXPERTS_PER_GROUP = 8
N_EXPERTS = MOE_GROUPS * EXPERTS_PER_GROUP
TOP_K = 2
D_EXPERT = D_MODEL // 4
EPS = 1e-6

LANES = 128
SUBLANES = 8
BF16_ROWS = 16
VMEM_LIMIT_BYTES = 56 * 1024 * 1024

ROW_TILE = 512
MM_TM = 1024
MM_TN = 1024
MM_TN_OUT = 512
RET_CHUNK = 128
RET_TB = 512
RET_SAMPLE_BB = 8
GDN_CHUNK = 64
GDN_TB = 256
MOE_BLOCK = 128
COMBINE_TM = 128


def _params(sem):
    return pltpu.CompilerParams(dimension_semantics=sem, vmem_limit_bytes=VMEM_LIMIT_BYTES)


def _dot(a, b):
    return jnp.dot(a, b, preferred_element_type=F32)


def _dot_nt(a, b):
    return lax.dot_general(a, b, (((1,), (1,)), ((), ())), preferred_element_type=F32)


def _dot_tn(a, b):
    return lax.dot_general(a, b, (((0,), (0,)), ((), ())), preferred_element_type=F32)


def _split3(x):
    hi = x.astype(BF16)
    r1 = x - hi.astype(F32)
    mid = r1.astype(BF16)
    lo = (r1 - mid.astype(F32)).astype(BF16)
    return hi, mid, lo


def _silu(x):
    return x * (1.0 / (1.0 + jnp.exp(-x)))


def _rmsnorm_kernel(x_ref, w_ref, o_ref):
    x = x_ref[...]
    ms = jnp.mean(x * x, axis=-1, keepdims=True)
    o_ref[...] = (x * lax.rsqrt(ms + EPS) * w_ref[...]).astype(o_ref.dtype)


def _rmsnorm(x, w, out_dtype):
    t, d = x.shape
    return pl.pallas_call(
        _rmsnorm_kernel,
        grid=(t // ROW_TILE,),
        in_specs=[pl.BlockSpec((ROW_TILE, d), lambda i: (i, 0)),
                  pl.BlockSpec((1, d), lambda i: (0, 0))],
        out_specs=pl.BlockSpec((ROW_TILE, d), lambda i: (i, 0)),
        out_shape=jax.ShapeDtypeStruct((t, d), out_dtype),
        compiler_params=_params(("parallel",)),
        name="rmsnorm",
    )(x, w.reshape(1, d))


def _matmul_kernel(a_ref, w_ref, *rest, has_res):
    if has_res:
        r_ref, o_ref, wb_ref = rest
    else:
        o_ref, wb_ref = rest

    @pl.when(pl.program_id(1) == 0)
    def _():
        wb_ref[...] = w_ref[...].astype(BF16)

    acc = _dot(a_ref[...], wb_ref[...])
    if has_res:
        acc = acc + r_ref[...]
    o_ref[...] = acc


def _matmul(a, w, n_cols, tm, tn, res=None):
    m, k = a.shape
    in_specs = [pl.BlockSpec((tm, k), lambda j, i: (i, 0)),
                pl.BlockSpec((k, tn), lambda j, i: (0, j))]
    args = [a, w]
    if res is not None:
        in_specs.append(pl.BlockSpec((tm, tn), lambda j, i: (i, j)))
        args.append(res)
    return pl.pallas_call(
        functools.partial(_matmul_kernel, has_res=res is not None),
        grid=(n_cols // tn, m // tm),
        in_specs=in_specs,
        out_specs=pl.BlockSpec((tm, tn), lambda j, i: (i, j)),
        out_shape=jax.ShapeDtypeStruct((m, n_cols), F32),
        scratch_shapes=[pltpu.VMEM((k, tn), BF16)],
        compiler_params=_params(("arbitrary", "arbitrary")),
        name="matmul",
    )(*args)


def _ret_kernel(lg_ref, q_ref, k_ref, v_ref, g_ref, cos_ref, sin_ref, gn_ref, *rest,
                chunk, n_seq, chunks_per_seq, has_s0):
    if has_s0:
        s0_ref, o_ref, s_ref = rest
    else:
        o_ref, s_ref = rest
    c = chunk
    h = pl.program_id(1)
    lg = lg_ref[h]

    @pl.when(pl.program_id(2) == 0)
    def _():
        if has_s0:
            s_ref[...] = s0_ref[...]
        else:
            s_ref[...] = jnp.zeros(s_ref.shape, F32)

    ii = lax.broadcasted_iota(jnp.int32, (c, 1), 0).astype(F32)
    jj = lax.broadcasted_iota(jnp.int32, (1, c), 1).astype(F32)
    diff = ii - jj
    causal = diff >= 0.0
    decay = jnp.where(causal, jnp.exp(lg * jnp.where(causal, diff, 0.0)), 0.0)
    q_decay = jnp.exp(lg * (ii + 1.0))
    k_decay = jnp.exp(lg * (c - 1.0 - ii))
    chunk_decay = jnp.exp(jnp.full((1, 1), lg * c, F32))
    half = RET_DK // 2
    k_scale = RET_DK ** -0.5
    gn_w = gn_ref[...]
    op = (lambda t: t.astype(BF16)) if c % BF16_ROWS == 0 else (lambda t: t)
    outs = []

    for s in range(n_seq):
        state = s_ref[s]
        for ci in range(chunks_per_seq):
            r0 = (s * chunks_per_seq + ci) * c
            rows = slice(r0, r0 + c)
            cos = cos_ref[rows, :]
            sin = sin_ref[rows, :]
            q = q_ref[rows, :]
            k = k_ref[rows, :]
            q1, q2 = q[:, :half], q[:, half:]
            k1, k2 = k[:, :half], k[:, half:]
            qr = jnp.concatenate([q1 * cos - q2 * sin, q1 * sin + q2 * cos], axis=-1)
            kr = jnp.concatenate([k1 * cos - k2 * sin, k1 * sin + k2 * cos], axis=-1) * k_scale
            vb = op(v_ref[rows, :])
            qb = op(qr)
            scores = _dot_nt(qb, op(kr)) * decay
            inner = _dot(op(scores), vb)
            cross = _dot(qb, op(state)) * q_decay
            o = inner + cross
            state = state * chunk_decay + _dot_tn(op(kr * k_decay), vb)
            mu = jnp.mean(o, axis=-1, keepdims=True)
            oc = o - mu
            var = jnp.mean(oc * oc, axis=-1, keepdims=True)
            on = oc * lax.rsqrt(var + EPS) * gn_w
            res = on * _silu(g_ref[rows, :])
            if c % BF16_ROWS == 0:
                o_ref[rows, :] = res.astype(o_ref.dtype)
            else:
                outs.append(res)
        s_ref[s] = state
    if outs:
        o_ref[...] = jnp.concatenate(outs, axis=0).astype(o_ref.dtype)


def _retention(proj, row0, n_seq_total, seq_len, pos0, s0, gn_w, *, chunk, tb, bb):
    hq = RET_HEADS * RET_DK
    n_rows = n_seq_total * seq_len
    blk_rows = bb * tb
    n_inner = seq_len // tb
    n_outer = n_seq_total // bb
    assert row0 % blk_rows == 0 and seq_len % tb == 0 and tb % chunk == 0
    rb0 = row0 // blk_rows
    half = RET_DK // 2

    pos = (pos0 + jnp.arange(seq_len)).astype(F32)
    inv = ROPE_BASE ** (-jnp.arange(half, dtype=F32) / half)
    ang = pos[:, None] * inv[None, :]
    cos = jnp.tile(jnp.cos(ang), (bb, 1)) if n_inner == 1 else jnp.cos(ang)
    sin = jnp.tile(jnp.sin(ang), (bb, 1)) if n_inner == 1 else jnp.sin(ang)
    log_gamma = jnp.log1p(-jnp.exp2(-5.0 - jnp.arange(RET_HEADS, dtype=F32)))

    def row_map(b, h, n):
        return rb0 + b * n_inner + n

    in_specs = [
        pl.BlockSpec(memory_space=pltpu.SMEM),
        pl.BlockSpec((blk_rows, RET_DK), lambda b, h, n: (row_map(b, h, n), h)),
        pl.BlockSpec((blk_rows, RET_DK), lambda b, h, n: (row_map(b, h, n), RET_HEADS + h)),
        pl.BlockSpec((blk_rows, RET_DV), lambda b, h, n: (row_map(b, h, n), 2 * hq // RET_DV + h)),
        pl.BlockSpec((blk_rows, RET_DV), lambda b, h, n: (row_map(b, h, n), 2 * hq // RET_DV + RET_HEADS + h)),
        pl.BlockSpec((blk_rows, half), lambda b, h, n: (n, 0)),
        pl.BlockSpec((blk_rows, half), lambda b, h, n: (n, 0)),
        pl.BlockSpec((None, 1, RET_DV), lambda b, h, n: (h, 0, 0)),
    ]
    args = [log_gamma, proj, proj, proj, proj, cos, sin, gn_w.reshape(RET_HEADS, 1, RET_DV)]
    if s0 is not None:
        in_specs.append(pl.BlockSpec((bb, None, RET_DK, RET_DV), lambda b, h, n: (b, h, 0, 0)))
        args.append(s0)
    out, s_fin = pl.pallas_call(
        functools.partial(_ret_kernel, chunk=chunk, n_seq=bb, chunks_per_seq=tb // chunk,
                          has_s0=s0 is not None),
        grid=(n_outer, RET_HEADS, n_inner),
        in_specs=in_specs,
        out_specs=[pl.BlockSpec((blk_rows, RET_DV), lambda b, h, n: (b * n_inner + n, h)),
                   pl.BlockSpec((bb, None, RET_DK, RET_DV), lambda b, h, n: (b, h, 0, 0))],
        out_shape=[jax.ShapeDtypeStruct((n_rows, RET_HEADS * RET_DV), BF16),
                   jax.ShapeDtypeStruct((n_seq_total, RET_HEADS, RET_DK, RET_DV), F32)],
        compiler_params=_params(("parallel", "parallel", "arbitrary")),
        name="retention",
    )(*args)
    return out, s_fin


def _shift_rows(x, prev8, s):
    rolled = pltpu.roll(x, s, 0)
    head = pltpu.roll(prev8, s, 0)
    row = lax.broadcasted_iota(jnp.int32, (SUBLANES, x.shape[1]), 0)
    top = jnp.where(row < s, head, rolled[:SUBLANES])
    if x.shape[0] == SUBLANES:
        return top
    return jnp.concatenate([top, rolled[SUBLANES:]], axis=0)


def _causal_conv_silu(x, prev8, cw):
    acc = x * cw[CONV_W - 1:CONV_W, :]
    for s in range(1, CONV_W):
        acc = acc + _shift_rows(x, prev8, s) * cw[CONV_W - 1 - s:CONV_W - s, :]
    return _silu(acc)


def _gdn_kernel(q_ref, k_ref, v_ref, z_ref, gb_ref, alog_ref, dtb_ref, cq_ref, ck_ref, cv_ref,
                nw_ref, *rest, chunk, n_seq, rows_per_seq, carry, has_init):
    if has_init:
        pq_ref, pk_ref, pv_ref, s0_ref, o_ref, s_ref, prev_ref = rest
    else:
        o_ref, s_ref, prev_ref = rest
    c = chunk
    r = n_seq * rows_per_seq
    n_chunks = r // c
    dk = GDN_DK
    op = (lambda t: t.astype(BF16)) if c % BF16_ROWS == 0 else (lambda t: t)

    @pl.when(pl.program_id(2) == 0)
    def _():
        if has_init:
            s_ref[...] = s0_ref[...]
        else:
            s_ref[...] = jnp.zeros(s_ref.shape, F32)
            prev_ref[...] = jnp.zeros(prev_ref.shape, F32)

    q_raw, k_raw, v_raw = q_ref[...], k_ref[...], v_ref[...]
    if carry:
        prev = prev_ref[...]
        qc = _causal_conv_silu(q_raw, prev[:, :dk], cq_ref[...])
        kc = _causal_conv_silu(k_raw, prev[:, dk:2 * dk], ck_ref[...])
        vc = _causal_conv_silu(v_raw, prev[:, 2 * dk:], cv_ref[...])
        prev_ref[...] = jnp.concatenate(
            [q_raw[r - SUBLANES:], k_raw[r - SUBLANES:], v_raw[r - SUBLANES:]], axis=-1)
    else:
        qs, ks, vs = [], [], []
        for s in range(n_seq):
            rows = slice(s * rows_per_seq, (s + 1) * rows_per_seq)
            qs.append(_causal_conv_silu(q_raw[rows], pq_ref[s], cq_ref[...]))
            ks.append(_causal_conv_silu(k_raw[rows], pk_ref[s], ck_ref[...]))
            vs.append(_causal_conv_silu(v_raw[rows], pv_ref[s], cv_ref[...]))
        qc = jnp.concatenate(qs, axis=0)
        kc = jnp.concatenate(ks, axis=0)
        vc = jnp.concatenate(vs, axis=0)

    qn = qc * lax.rsqrt(jnp.sum(qc * qc, axis=-1, keepdims=True) + EPS) * (dk ** -0.5)
    kn = kc * lax.rsqrt(jnp.sum(kc * kc, axis=-1, keepdims=True) + EPS)
    qb = qn.astype(BF16)
    kb = kn.astype(BF16)

    gb = gb_ref[...]
    a_coef = jnp.exp(alog_ref[...][:, :1])
    xs = gb + dtb_ref[...][:, :1]
    softplus = jnp.maximum(xs, 0.0) + jnp.log(1.0 + jnp.exp(-jnp.abs(xs)))
    g_rows = -a_coef * softplus
    beta_rows = 1.0 / (1.0 + jnp.exp(-gb))
    rid = lax.broadcasted_iota(jnp.int32, gb.shape, 0)
    x8 = jnp.where(rid < 2, g_rows, beta_rows)

    ri = lax.broadcasted_iota(jnp.int32, (r, r), 0)
    ci = lax.broadcasted_iota(jnp.int32, (r, r), 1)
    log2c = int(math.log2(c))
    same = lax.shift_right_logical(ri, log2c) == lax.shift_right_logical(ci, log2c)
    tri_incl = jnp.logical_and(same, ri >= ci)
    tri_strict = jnp.logical_and(same, ri > ci)
    tri_b = jnp.where(tri_incl, 1.0, 0.0).astype(BF16)
    eye_b = jnp.where(ri == ci, 1.0, 0.0).astype(BF16)
    eye_f = jnp.where(ri == ci, 1.0, 0.0)

    parts = _split3(x8)
    cs_col = sum(_dot_nt(tri_b, p) for p in parts)
    cs_row = sum(_dot_nt(p, tri_b) for p in parts)
    x_col = sum(_dot_nt(eye_b, p) for p in parts)
    exp_col = jnp.exp(cs_col)

    kk = _dot_nt(kb, kb)
    qk = _dot_nt(qb, kb)
    nw = nw_ref[...]

    for vh in range(2):
        g_col = cs_col[:, vh:vh + 1]
        g_row = cs_row[vh:vh + 1, :]
        b_col = x_col[:, 2 + vh:3 + vh]
        eg_col = exp_col[:, vh:vh + 1]
        dmat = jnp.exp(jnp.where(tri_incl, g_col - g_row, -jnp.inf))
        a_mat = jnp.where(tri_strict, -(kk * dmat) * b_col, 0.0)
        t_mat = eye_f + a_mat
        pw = a_mat
        n_sq = max(log2c - 1, 0)
        for _ in range(n_sq):
            pwb = pw.astype(BF16)
            pw = _dot(pwb, pwb)
            t_mat = t_mat + _dot(t_mat.astype(BF16), pw.astype(BF16))
        v_h = vc[:, vh * GDN_DV:(vh + 1) * GDN_DV]
        rhs = jnp.concatenate([v_h * b_col, kn * (b_col * eg_col)], axis=-1)
        sol = _dot(t_mat.astype(BF16), rhs.astype(BF16))
        u_all = sol[:, :GDN_DV]
        w_all = sol[:, GDN_DV:]
        q_in = qn * eg_col

        v_new_parts, o_inter_parts = [], []
        state = None
        for ch in range(n_chunks):
            rows = slice(ch * c, (ch + 1) * c)
            seq = (ch * c) // rows_per_seq
            if state is None or (ch * c) % rows_per_seq == 0:
                if state is not None:
                    s_ref[seq - 1, vh] = state
                state = s_ref[seq, vh]
            sb = op(state)
            v_new = u_all[rows] - _dot(op(w_all[rows]), sb)
            o_inter_parts.append(_dot(op(q_in[rows]), sb))
            g_last = g_col[(ch + 1) * c - 1:(ch + 1) * c, :]
            k_dec = op(kn[rows] * jnp.exp(g_last - g_col[rows]))
            state = state * jnp.exp(g_last) + _dot_tn(k_dec, op(v_new))
            v_new_parts.append(v_new)
        s_ref[n_seq - 1, vh] = state
        v_new_all = jnp.concatenate(v_new_parts, axis=0) if n_chunks > 1 else v_new_parts[0]
        o_inter = jnp.concatenate(o_inter_parts, axis=0) if n_chunks > 1 else o_inter_parts[0]
        attn = jnp.where(tri_incl, qk * dmat, 0.0)
        o = o_inter + _dot(attn.astype(BF16), v_new_all.astype(BF16))
        o = o * lax.rsqrt(jnp.mean(o * o, axis=-1, keepdims=True) + EPS) * nw
        z = z_ref[:, vh * GDN_DV:(vh + 1) * GDN_DV]
        o_ref[:, vh * GDN_DV:(vh + 1) * GDN_DV] = (o * _silu(z)).astype(o_ref.dtype)


def _gdn(proj, gates, row0, n_seq_total, seq_len, conv0_pad, s0, conv_w, a_log, dt_bias, norm_w,
         *, chunk, tb, bb):
    hk = GDN_K_HEADS * GDN_DK
    hv = GDN_V_HEADS * GDN_DV
    n_rows = n_seq_total * seq_len
    blk_rows = bb * tb
    n_inner = seq_len // tb
    n_outer = n_seq_total // bb
    carry = s0 is None
    assert row0 % blk_rows == 0 and seq_len % tb == 0 and tb % chunk == 0
    assert carry == (bb == 1)
    rb0 = row0 // blk_rows
    two = 2 * GDN_DV

    def row_map(b, h, n):
        return rb0 + b * n_inner + n

    alog8 = jnp.zeros((GDN_K_HEADS, SUBLANES, LANES), F32).at[:, :2, :].set(
        jnp.broadcast_to(a_log.reshape(GDN_K_HEADS, 2, 1), (GDN_K_HEADS, 2, LANES)))
    dtb8 = jnp.zeros((GDN_K_HEADS, SUBLANES, LANES), F32).at[:, :2, :].set(
        jnp.broadcast_to(dt_bias.reshape(GDN_K_HEADS, 2, 1), (GDN_K_HEADS, 2, LANES)))

    in_specs = [
        pl.BlockSpec((blk_rows, GDN_DK), lambda b, h, n: (row_map(b, h, n), h)),
        pl.BlockSpec((blk_rows, GDN_DK), lambda b, h, n: (row_map(b, h, n), GDN_K_HEADS + h)),
        pl.BlockSpec((blk_rows, two), lambda b, h, n: (row_map(b, h, n), 2 * hk // two + h)),
        pl.BlockSpec((blk_rows, two), lambda b, h, n: (row_map(b, h, n), CONV_DIM // two + h)),
        pl.BlockSpec((None, SUBLANES, blk_rows), lambda b, h, n: (h, 0, row_map(b, h, n))),
        pl.BlockSpec((None, SUBLANES, LANES), lambda b, h, n: (h, 0, 0)),
        pl.BlockSpec((None, SUBLANES, LANES), lambda b, h, n: (h, 0, 0)),
        pl.BlockSpec((CONV_W, GDN_DK), lambda b, h, n: (0, h)),
        pl.BlockSpec((CONV_W, GDN_DK), lambda b, h, n: (0, GDN_K_HEADS + h)),
        pl.BlockSpec((CONV_W, two), lambda b, h, n: (0, 2 * hk // two + h)),
        pl.BlockSpec((1, GDN_DV), lambda b, h, n: (0, 0)),
    ]
    args = [proj, proj, proj, proj, gates, alog8, dtb8, conv_w, conv_w, conv_w,
            norm_w.reshape(1, GDN_DV)]
    if not carry:
        in_specs += [
            pl.BlockSpec((bb, SUBLANES, GDN_DK), lambda b, h, n: (b, 0, h)),
            pl.BlockSpec((bb, SUBLANES, GDN_DK), lambda b, h, n: (b, 0, GDN_K_HEADS + h)),
            pl.BlockSpec((bb, SUBLANES, two), lambda b, h, n: (b, 0, 2 * hk // two + h)),
            pl.BlockSpec((bb, 2, GDN_DK, GDN_DV), lambda b, h, n: (b, h, 0, 0)),
        ]
        args += [conv0_pad, conv0_pad, conv0_pad, s0]
    out, s_fin = pl.pallas_call(
        functools.partial(_gdn_kernel, chunk=chunk, n_seq=bb, rows_per_seq=tb, carry=carry,
                          has_init=not carry),
        grid=(n_outer, GDN_K_HEADS, n_inner),
        in_specs=in_specs,
        out_specs=[pl.BlockSpec((blk_rows, two), lambda b, h, n: (b * n_inner + n, h)),
                   pl.BlockSpec((bb, 2, GDN_DK, GDN_DV), lambda b, h, n: (b, h, 0, 0))],
        out_shape=[jax.ShapeDtypeStruct((n_rows, hv), BF16),
                   jax.ShapeDtypeStruct((n_seq_total, GDN_V_HEADS, GDN_DK, GDN_DV), F32)],
        scratch_shapes=[pltpu.VMEM((SUBLANES, 2 * GDN_DK + two), F32)],
        compiler_params=_params(("parallel", "parallel", "arbitrary")),
        name="gated_delta",
    )(*args)
    return out, s_fin


def _router_kernel(x_ref, w_ref, wr_ref, br_ref, h_ref, lg_ref):
    x = x_ref[...]
    ms = jnp.mean(x * x, axis=-1, keepdims=True)
    h = x * lax.rsqrt(ms + EPS) * w_ref[...]
    h_ref[...] = h
    h_hi, h_mid, _ = _split3(h)
    w = wr_ref[...]
    w_hi = w.astype(BF16)
    w_mid = (w - w_hi.astype(F32)).astype(BF16)
    lg_ref[...] = _dot(h_hi, w_hi) + _dot(h_hi, w_mid) + _dot(h_mid, w_hi) + br_ref[...]


def _router(x, norm_w, w_route, b_route):
    t, d = x.shape
    return pl.pallas_call(
        _router_kernel,
        grid=(t // ROW_TILE,),
        in_specs=[pl.BlockSpec((ROW_TILE, d), lambda i: (i, 0)),
                  pl.BlockSpec((1, d), lambda i: (0, 0)),
                  pl.BlockSpec((d, LANES), lambda i: (0, 0)),
                  pl.BlockSpec((1, LANES), lambda i: (0, 0))],
        out_specs=[pl.BlockSpec((ROW_TILE, d), lambda i: (i, 0)),
                   pl.BlockSpec((ROW_TILE, LANES), lambda i: (i, 0))],
        out_shape=[jax.ShapeDtypeStruct((t, d), F32),
                   jax.ShapeDtypeStruct((t, LANES), F32)],
        compiler_params=_params(("parallel",)),
        name="moe_router",
    )(x, norm_w.reshape(1, d), w_route, b_route)


def _row_gather_start(idx_ref, base, n, src_hbm, dst_ref, sem):
    def body(r, carry):
        tok = idx_ref[base + r]
        pltpu.make_async_copy(src_hbm.at[pl.ds(tok, 1)], dst_ref.at[pl.ds(r, 1)], sem).start()
        return carry
    lax.fori_loop(0, n, body, 0)


def _row_gather_wait(n, src_hbm, dst_ref, sem):
    pltpu.make_async_copy(src_hbm.at[pl.ds(0, n)], dst_ref, sem).wait()


def _moe_ffn_kernel(be_ref, rt_ref, na_ref, h_hbm, rw_ref, win_ref, wout_ref, o_ref,
                    xbuf, sem, winb, woutb):
    i = pl.program_id(0)
    n_active = na_ref[0]
    slot = lax.rem(i, 2)

    @pl.when(jnp.logical_and(i == 0, n_active > 0))
    def _():
        _row_gather_start(rt_ref, 0, MOE_BLOCK, h_hbm, xbuf.at[0], sem.at[0])

    @pl.when(i + 1 < n_active)
    def _():
        _row_gather_start(rt_ref, (i + 1) * MOE_BLOCK, MOE_BLOCK, h_hbm,
                          xbuf.at[1 - slot], sem.at[1 - slot])

    @pl.when(i < n_active)
    def _():
        prev_e = be_ref[jnp.maximum(i - 1, 0)]

        @pl.when(jnp.logical_or(i == 0, be_ref[i] != prev_e))
        def _():
            winb[...] = win_ref[...].astype(BF16)
            woutb[...] = wout_ref[...].astype(BF16)

        _row_gather_wait(MOE_BLOCK, h_hbm, xbuf.at[slot], sem.at[slot])
        x = xbuf[slot].astype(BF16)
        mid = _dot(x, winb[...])
        act = (_silu(mid[:, :D_EXPERT]) * mid[:, D_EXPERT:]).astype(BF16)
        o_ref[...] = _dot(act, woutb[...]) * rw_ref[...]

    @pl.when(i >= n_active)
    def _():
        o_ref[...] = jnp.zeros(o_ref.shape, F32)


def _moe_ffn(h, block_e, row_tok, n_active, row_w, w_in, w_out, layer):
    t, d = h.shape
    n_blocks = block_e.shape[0]
    n_rows = n_blocks * MOE_BLOCK
    grid_spec = pltpu.PrefetchScalarGridSpec(
        num_scalar_prefetch=3,
        grid=(n_blocks,),
        in_specs=[
            pl.BlockSpec(memory_space=pl.ANY),
            pl.BlockSpec((MOE_BLOCK, 1), lambda i, be, rt, na: (i, 0)),
            pl.BlockSpec((None, None, d, 2 * D_EXPERT), lambda i, be, rt, na: (layer, be[i], 0, 0)),
            pl.BlockSpec((None, None, D_EXPERT, d), lambda i, be, rt, na: (layer, be[i], 0, 0)),
        ],
        out_specs=pl.BlockSpec((MOE_BLOCK, d), lambda i, be, rt, na: (i, 0)),
        scratch_shapes=[pltpu.VMEM((2, MOE_BLOCK, d), F32),
                        pltpu.SemaphoreType.DMA((2,)),
                        pltpu.VMEM((d, 2 * D_EXPERT), BF16),
                        pltpu.VMEM((D_EXPERT, d), BF16)],
    )
    return pl.pallas_call(
        _moe_ffn_kernel,
        grid_spec=grid_spec,
        out_shape=jax.ShapeDtypeStruct((n_rows, d), F32),
        compiler_params=_params(("arbitrary",)),
        name="moe_ffn",
    )(block_e, row_tok, n_active, h, row_w.reshape(n_rows, 1), w_in, w_out)


def _combine_kernel(pos_ref, ys_hbm, x_ref, o_ref, buf, sem):
    i = pl.program_id(0)
    n = pl.num_programs(0)
    slot = lax.rem(i, 2)
    rows = 2 * COMBINE_TM

    @pl.when(i == 0)
    def _():
        _row_gather_start(pos_ref, 0, rows, ys_hbm, buf.at[0], sem.at[0])

    @pl.when(i + 1 < n)
    def _():
        _row_gather_start(pos_ref, (i + 1) * rows, rows, ys_hbm, buf.at[1 - slot], sem.at[1 - slot])

    _row_gather_wait(rows, ys_hbm, buf.at[slot], sem.at[slot])
    o_ref[...] = x_ref[...] + buf[slot, :COMBINE_TM, :] + buf[slot, COMBINE_TM:, :]


def _combine(x, ys, pos):
    t, d = x.shape
    grid_spec = pltpu.PrefetchScalarGridSpec(
        num_scalar_prefetch=1,
        grid=(t // COMBINE_TM,),
        in_specs=[pl.BlockSpec(memory_space=pl.ANY),
                  pl.BlockSpec((COMBINE_TM, d), lambda i, p: (i, 0))],
        out_specs=pl.BlockSpec((COMBINE_TM, d), lambda i, p: (i, 0)),
        scratch_shapes=[pltpu.VMEM((2, 2 * COMBINE_TM, d), F32),
                        pltpu.SemaphoreType.DMA((2,))],
    )
    return pl.pallas_call(
        _combine_kernel,
        grid_spec=grid_spec,
        out_shape=jax.ShapeDtypeStruct((t, d), F32),
        compiler_params=_params(("arbitrary",)),
        name="moe_combine",
    )(pos, ys, x)


def _hier_moe(x, norm_w, w_group, b_group, w_expert, b_expert, w_in, w_out, layer):
    t, d = x.shape
    n_route = MOE_GROUPS + N_EXPERTS
    w_route = jnp.zeros((d, LANES), F32).at[:, :MOE_GROUPS].set(w_group).at[:, MOE_GROUPS:n_route].set(w_expert)
    b_route = jnp.zeros((1, LANES), F32).at[0, :MOE_GROUPS].set(b_group).at[0, MOE_GROUPS:n_route].set(b_expert)
    h, logits = _router(x, norm_w, w_route, b_route)

    g_logits = logits[:, :MOE_GROUPS]
    g_sel = jnp.argmax(g_logits, axis=-1)
    g_w = jnp.take_along_axis(jax.nn.softmax(g_logits, axis=-1), g_sel[:, None], axis=1)
    e_logits = logits[:, MOE_GROUPS:n_route].reshape(t, MOE_GROUPS, EXPERTS_PER_GROUP)
    e_logits = jnp.take_along_axis(e_logits, g_sel[:, None, None], axis=1)[:, 0]
    top_v, top_i = lax.top_k(e_logits, TOP_K)
    weights = g_w * jax.nn.softmax(top_v, axis=-1)
    expert_id = (g_sel[:, None] * EXPERTS_PER_GROUP + top_i).astype(jnp.int32)

    n_assign = t * TOP_K
    flat_e = expert_id.reshape(n_assign)
    onehot = (flat_e[:, None] == jnp.arange(N_EXPERTS, dtype=jnp.int32)[None, :]).astype(jnp.int32)
    csum = jnp.cumsum(onehot, axis=0)
    rank = jnp.sum(onehot * (csum - 1), axis=1)
    counts = csum[-1]
    padded = ((counts + MOE_BLOCK - 1) // MOE_BLOCK) * MOE_BLOCK
    pad_end = jnp.cumsum(padded)
    pad_start = pad_end - padded
    dest = (pad_start[flat_e] + rank).astype(jnp.int32)
    n_blocks = -(-n_assign // MOE_BLOCK) + N_EXPERTS
    n_rows = n_blocks * MOE_BLOCK
    tok_of = (jnp.arange(n_assign, dtype=jnp.int32) // TOP_K)
    row_tok = jnp.zeros((n_rows,), jnp.int32).at[dest].set(tok_of)
    row_w = jnp.zeros((n_rows,), F32).at[dest].set(weights.reshape(n_assign))
    n_active = (pad_end[-1] // MOE_BLOCK).astype(jnp.int32)
    blk = jnp.arange(n_blocks, dtype=jnp.int32)
    blk = jnp.minimum(blk, jnp.maximum(n_active - 1, 0))
    block_e = jnp.minimum(jnp.sum((blk * MOE_BLOCK)[:, None] >= pad_end[None, :], axis=1),
                          N_EXPERTS - 1).astype(jnp.int32)

    ys = _moe_ffn(h, block_e, row_tok, n_active.reshape(1), row_w, w_in, w_out, layer)
    pos = dest.reshape(t // COMBINE_TM, COMBINE_TM, TOP_K).transpose(0, 2, 1).reshape(n_assign)
    return _combine(x, ys, pos)


def kernel(x_prompt, x_sample, state_ret, state_gdn, state_conv, norm_mix, norm_ffn, norm_final,
           ret_w_in, ret_gn, ret_w_out, gdn_w_in, gdn_conv, gdn_a_log, gdn_dt_bias, gdn_norm,
           gdn_w_out, moe_w_group, moe_b_group, moe_w_expert, moe_b_expert, moe_w_in, moe_w_out):
    bp, lp, d = x_prompt.shape
    bs, ls, _ = x_sample.shape
    tp, ts = bp * lp, bs * ls
    x = jnp.concatenate([x_prompt.reshape(tp, d), x_sample.reshape(ts, d)], axis=0)

    h = _rmsnorm(x, norm_mix[0], BF16)
    ret_in = ret_w_in.shape[-1]
    proj = _matmul(h, ret_w_in.reshape(d, ret_in), ret_in, MM_TM, MM_TN)
    o_p, ret_p = _retention(proj, 0, bp, lp, 0, None, ret_gn[0],
                            chunk=RET_CHUNK, tb=RET_TB, bb=1)
    o_s, ret_s = _retention(proj, tp, bs, ls, PAST_LEN, state_ret[0], ret_gn[0],
                            chunk=ls, tb=ls, bb=RET_SAMPLE_BB)
    o = jnp.concatenate([o_p, o_s], axis=0)
    x = _matmul(o, ret_w_out.reshape(o.shape[1], d), d, ROW_TILE, MM_TN_OUT, res=x)
    x = _hier_moe(x, norm_ffn[0], moe_w_group[0], moe_b_group[0], moe_w_expert[0], moe_b_expert[0],
                  moe_w_in, moe_w_out, 0)

    h = _rmsnorm(x, norm_mix[1], BF16)
    hv = GDN_V_HEADS * GDN_DV
    n_main = CONV_DIM + hv
    w_gdn = gdn_w_in.reshape(d, gdn_w_in.shape[-1])
    proj = _matmul(h, w_gdn, n_main, MM_TM, MM_TN)
    w_tail = jnp.zeros((d, LANES), F32).at[:, :2 * GDN_V_HEADS].set(w_gdn[:, n_main:])
    tail = _matmul(h, w_tail, LANES, MM_TM, LANES)
    t = tp + ts
    bt = tail[:, :GDN_V_HEADS].reshape(t, GDN_K_HEADS, 2)
    a = tail[:, GDN_V_HEADS:2 * GDN_V_HEADS].reshape(t, GDN_K_HEADS, 2)
    gates = jnp.concatenate([a, bt, jnp.zeros((t, GDN_K_HEADS, 4), F32)], axis=-1).transpose(1, 2, 0)
    conv0_pad = jnp.pad(state_conv[0], ((0, 0), (SUBLANES - (CONV_W - 1), 0), (0, 0)))
    o_p, gdn_p = _gdn(proj, gates, 0, bp, lp, None, None, gdn_conv[0], gdn_a_log[0], gdn_dt_bias[0],
                      gdn_norm[0], chunk=GDN_CHUNK, tb=GDN_TB, bb=1)
    o_s, gdn_s = _gdn(proj, gates, tp, bs, ls, conv0_pad, state_gdn[0], gdn_conv[0], gdn_a_log[0],
                      gdn_dt_bias[0], gdn_norm[0], chunk=ls, tb=ls, bb=GDN_TB // ls)
    conv_p = proj[:tp, :CONV_DIM].reshape(bp, lp, CONV_DIM)[:, lp - (CONV_W - 1):]
    conv_s = proj[tp:, :CONV_DIM].reshape(bs, ls, CONV_DIM)[:, ls - (CONV_W - 1):]
    o = jnp.concatenate([o_p, o_s], axis=0)
    x = _matmul(o, gdn_w_out.reshape(hv, d), d, ROW_TILE, MM_TN_OUT, res=x)
    x = _hier_moe(x, norm_ffn[1], moe_w_group[1], moe_b_group[1], moe_w_expert[1], moe_b_expert[1],
                  moe_w_in, moe_w_out, 1)

    y = _rmsnorm(x, norm_final, F32)
    return (y[:tp].reshape(bp, lp, d), y[tp:].reshape(bs, ls, d),
            ret_p[None], ret_s[None], gdn_p[None], gdn_s[None], conv_p[None], conv_s[None])
```

```python
import functools
import math

import numpy as np
import jax
import jax.numpy as jnp
from jax import lax
from jax.experimental import pallas as pl
from jax.experimental.pallas import tpu as pltpu

F32 = jnp.float32
BF16 = jnp.bfloat16

D_MODEL = 2048
PAST_LEN = 16384
RET_HEADS = 8
RET_DK = 256
RET_DV = 512
ROPE_BASE = 10000.0
GDN_K_HEADS = 16
GDN_V_HEADS = 32
GDN_DK = 128
GDN_DV = 128
CONV_W = 4
CONV_DIM = 2 * GDN_K_HEADS * GDN_DK + GDN_V_HEADS * GDN_DV
MOE_GROUPS = 8
EXPERTS_PER_GROUP = 8
N_EXPERTS = MOE_GROUPS * EXPERTS_PER_GROUP
TOP_K = 2
D_EXPERT = D_MODEL // 4
EPS = 1e-6

LANES = 128
SUBLANES = 8
BF16_ROWS = 16
VMEM_LIMIT_BYTES = 56 * 1024 * 1024

ROW_TILE = 512
MM_TM = 1024
MM_TN = 1024
MM_TN_OUT = 512
RET_CHUNK = 128
RET_TB = 512
RET_SAMPLE_BB = 8
GDN_CHUNK = 64
GDN_TB = 1024
GDN_GROUP = 128
GDN_SAMPLE_BB = 32
MOE_BLOCK = 128
COMBINE_TM = 128
GATHER_UNROLL = 8


def _params(sem):
    return pltpu.CompilerParams(dimension_semantics=sem, vmem_limit_bytes=VMEM_LIMIT_BYTES)


def _dot(a, b):
    return jnp.dot(a, b, preferred_element_type=F32)


def _dot_nt(a, b):
    return lax.dot_general(a, b, (((1,), (1,)), ((), ())), preferred_element_type=F32)


def _dot_tn(a, b):
    return lax.dot_general(a, b, (((0,), (0,)), ((), ())), preferred_element_type=F32)


def _split3(x):
    hi = x.astype(BF16)
    r1 = x - hi.astype(F32)
    mid = r1.astype(BF16)
    lo = (r1 - mid.astype(F32)).astype(BF16)
    return hi, mid, lo


def _silu(x):
    return x * (1.0 / (1.0 + jnp.exp(-x)))


def _rmsnorm_kernel(x_ref, w_ref, o_ref):
    x = x_ref[...]
    ms = jnp.mean(x * x, axis=-1, keepdims=True)
    o_ref[...] = (x * lax.rsqrt(ms + EPS) * w_ref[...]).astype(o_ref.dtype)


def _rmsnorm(x, w, out_dtype):
    t, d = x.shape
    return pl.pallas_call(
        _rmsnorm_kernel,
        grid=(t // ROW_TILE,),
        in_specs=[pl.BlockSpec((ROW_TILE, d), lambda i: (i, 0)),
                  pl.BlockSpec((1, d), lambda i: (0, 0))],
        out_specs=pl.BlockSpec((ROW_TILE, d), lambda i: (i, 0)),
        out_shape=jax.ShapeDtypeStruct((t, d), out_dtype),
        compiler_params=_params(("parallel",)),
        name="rmsnorm",
    )(x, w.reshape(1, d))


def _matmul_kernel(a_ref, w_ref, *rest, has_res):
    if has_res:
        r_ref, o_ref, wb_ref = rest
    else:
        o_ref, wb_ref = rest

    @pl.when(pl.program_id(1) == 0)
    def _():
        wb_ref[...] = w_ref[...].astype(BF16)

    acc = _dot(a_ref[...], wb_ref[...])
    if has_res:
        acc = acc + r_ref[...]
    o_ref[...] = acc


def _matmul(a, w, n_cols, tm, tn, res=None):
    m, k = a.shape
    in_specs = [pl.BlockSpec((tm, k), lambda j, i: (i, 0)),
                pl.BlockSpec((k, tn), lambda j, i: (0, j))]
    args = [a, w]
    if res is not None:
        in_specs.append(pl.BlockSpec((tm, tn), lambda j, i: (i, j)))
        args.append(res)
    return pl.pallas_call(
        functools.partial(_matmul_kernel, has_res=res is not None),
        grid=(n_cols // tn, m // tm),
        in_specs=in_specs,
        out_specs=pl.BlockSpec((tm, tn), lambda j, i: (i, j)),
        out_shape=jax.ShapeDtypeStruct((m, n_cols), F32),
        scratch_shapes=[pltpu.VMEM((k, tn), BF16)],
        compiler_params=_params(("arbitrary", "arbitrary")),
        name="matmul",
    )(*args)


def _ret_kernel(lg_ref, q_ref, k_ref, v_ref, g_ref, cos_ref, sin_ref, gn_ref, *rest,
                chunk, n_seq, chunks_per_seq, has_s0):
    if has_s0:
        s0_ref, o_ref, s_ref = rest
    else:
        o_ref, s_ref = rest
    c = chunk
    h = pl.program_id(1)
    lg = lg_ref[h]

    if not has_s0:
        @pl.when(pl.program_id(2) == 0)
        def _():
            s_ref[...] = jnp.zeros(s_ref.shape, F32)

    ii = lax.broadcasted_iota(jnp.int32, (c, 1), 0).astype(F32)
    jj = lax.broadcasted_iota(jnp.int32, (1, c), 1).astype(F32)
    diff = ii - jj
    causal = diff >= 0.0
    decay = jnp.where(causal, jnp.exp(lg * jnp.where(causal, diff, 0.0)), 0.0)
    q_decay = jnp.exp(lg * (ii + 1.0))
    k_decay = jnp.exp(lg * (c - 1.0 - ii))
    chunk_decay = jnp.exp(jnp.full((1, 1), lg * c, F32))
    half = RET_DK // 2
    k_scale = RET_DK ** -0.5
    gn_w = gn_ref[...]
    op = (lambda t: t.astype(BF16)) if c % BF16_ROWS == 0 else (lambda t: t)
    outs = []

    for s in range(n_seq):
        state = s0_ref[s] if has_s0 else s_ref[s]
        for ci in range(chunks_per_seq):
            r0 = (s * chunks_per_seq + ci) * c
            rows = slice(r0, r0 + c)
            cos = cos_ref[rows, :]
            sin = sin_ref[rows, :]
            q = q_ref[rows, :]
            k = k_ref[rows, :]
            q1, q2 = q[:, :half], q[:, half:]
            k1, k2 = k[:, :half], k[:, half:]
            qr = jnp.concatenate([q1 * cos - q2 * sin, q1 * sin + q2 * cos], axis=-1)
            kr = jnp.concatenate([k1 * cos - k2 * sin, k1 * sin + k2 * cos], axis=-1) * k_scale
            vb = op(v_ref[rows, :])
            qb = op(qr)
            scores = _dot_nt(qb, op(kr)) * decay
            inner = _dot(op(scores), vb)
            cross = _dot(qb, op(state)) * q_decay
            o = inner + cross
            state = state * chunk_decay + _dot_tn(op(kr * k_decay), vb)
            mu = jnp.mean(o, axis=-1, keepdims=True)
            oc = o - mu
            var = jnp.mean(oc * oc, axis=-1, keepdims=True)
            on = oc * lax.rsqrt(var + EPS) * gn_w
            res = on * _silu(g_ref[rows, :])
            if c % BF16_ROWS == 0:
                o_ref[rows, :] = res.astype(o_ref.dtype)
            else:
                outs.append(res)
        s_ref[s] = state
    if outs:
        o_ref[...] = jnp.concatenate(outs, axis=0).astype(o_ref.dtype)


def _retention(proj, row0, n_seq_total, seq_len, pos0, s0, gn_w, *, chunk, tb, bb):
    hq = RET_HEADS * RET_DK
    n_rows = n_seq_total * seq_len
    blk_rows = bb * tb
    n_inner = seq_len // tb
    n_outer = n_seq_total // bb
    assert row0 % blk_rows == 0 and seq_len % tb == 0 and tb % chunk == 0
    assert s0 is None or n_inner == 1
    rb0 = row0 // blk_rows
    half = RET_DK // 2

    pos = (pos0 + jnp.arange(seq_len)).astype(F32)
    inv = ROPE_BASE ** (-jnp.arange(half, dtype=F32) / half)
    ang = pos[:, None] * inv[None, :]
    cos = jnp.tile(jnp.cos(ang), (bb, 1)) if n_inner == 1 else jnp.cos(ang)
    sin = jnp.tile(jnp.sin(ang), (bb, 1)) if n_inner == 1 else jnp.sin(ang)
    log_gamma = jnp.log1p(-jnp.exp2(-5.0 - jnp.arange(RET_HEADS, dtype=F32)))

    def row_map(b, h, n):
        return rb0 + b * n_inner + n

    in_specs = [
        pl.BlockSpec(memory_space=pltpu.SMEM),
        pl.BlockSpec((blk_rows, RET_DK), lambda b, h, n: (row_map(b, h, n), h)),
        pl.BlockSpec((blk_rows, RET_DK), lambda b, h, n: (row_map(b, h, n), RET_HEADS + h)),
        pl.BlockSpec((blk_rows, RET_DV), lambda b, h, n: (row_map(b, h, n), 2 * hq // RET_DV + h)),
        pl.BlockSpec((blk_rows, RET_DV), lambda b, h, n: (row_map(b, h, n), 2 * hq // RET_DV + RET_HEADS + h)),
        pl.BlockSpec((blk_rows, half), lambda b, h, n: (n, 0)),
        pl.BlockSpec((blk_rows, half), lambda b, h, n: (n, 0)),
        pl.BlockSpec((None, 1, RET_DV), lambda b, h, n: (h, 0, 0)),
    ]
    args = [log_gamma, proj, proj, proj, proj, cos, sin, gn_w.reshape(RET_HEADS, 1, RET_DV)]
    if s0 is not None:
        in_specs.append(pl.BlockSpec((bb, None, RET_DK, RET_DV), lambda b, h, n: (b, h, 0, 0)))
        args.append(s0)
    out, s_fin = pl.pallas_call(
        functools.partial(_ret_kernel, chunk=chunk, n_seq=bb, chunks_per_seq=tb // chunk,
                          has_s0=s0 is not None),
        grid=(n_outer, RET_HEADS, n_inner),
        in_specs=in_specs,
        out_specs=[pl.BlockSpec((blk_rows, RET_DV), lambda b, h, n: (b * n_inner + n, h)),
                   pl.BlockSpec((bb, None, RET_DK, RET_DV), lambda b, h, n: (b, h, 0, 0))],
        out_shape=[jax.ShapeDtypeStruct((n_rows, RET_HEADS * RET_DV), BF16),
                   jax.ShapeDtypeStruct((n_seq_total, RET_HEADS, RET_DK, RET_DV), F32)],
        compiler_params=_params(("parallel", "parallel", "arbitrary")),
        name="retention",
    )(*args)
    return out, s_fin


def _shift_rows(x, prev8, s):
    rolled = pltpu.roll(x, s, 0)
    head = pltpu.roll(prev8, s, 0)
    row = lax.broadcasted_iota(jnp.int32, (SUBLANES, x.shape[1]), 0)
    top = jnp.where(row < s, head, rolled[:SUBLANES])
    if x.shape[0] == SUBLANES:
        return top
    return jnp.concatenate([top, rolled[SUBLANES:]], axis=0)


def _causal_conv_silu(x, prev8, cw):
    acc = x * cw[CONV_W - 1:CONV_W, :]
    for s in range(1, CONV_W):
        acc = acc + _shift_rows(x, prev8, s) * cw[CONV_W - 1 - s:CONV_W - s, :]
    return _silu(acc)


def _gdn_kernel(q_ref, k_ref, v_ref, z_ref, gb_ref, alog_ref, dtb_ref, cq_ref, ck_ref, cv_ref,
                nw_ref, *rest, chunk, group, n_seq, rows_per_seq, carry):
    has_init = not carry
    if has_init:
        pq_ref, pk_ref, pv_ref, s0_ref, o_ref, s_ref, prev_ref = rest
    else:
        o_ref, s_ref, prev_ref = rest
    c = chunk
    r = n_seq * rows_per_seq
    m = group
    dk = GDN_DK
    op = (lambda t: t.astype(BF16)) if c % BF16_ROWS == 0 else (lambda t: t)

    if carry:
        @pl.when(pl.program_id(2) == 0)
        def _():
            s_ref[...] = jnp.zeros(s_ref.shape, F32)
            prev_ref[...] = jnp.zeros(prev_ref.shape, F32)

    q_raw, k_raw, v_raw = q_ref[...], k_ref[...], v_ref[...]
    if carry:
        prev = prev_ref[...]
        qc = _causal_conv_silu(q_raw, prev[:, :dk], cq_ref[...])
        kc = _causal_conv_silu(k_raw, prev[:, dk:2 * dk], ck_ref[...])
        vc = _causal_conv_silu(v_raw, prev[:, 2 * dk:], cv_ref[...])
        prev_ref[...] = jnp.concatenate(
            [q_raw[r - SUBLANES:], k_raw[r - SUBLANES:], v_raw[r - SUBLANES:]], axis=-1)
    else:
        qs, ks, vs = [], [], []
        for s in range(n_seq):
            rows = slice(s * rows_per_seq, (s + 1) * rows_per_seq)
            qs.append(_causal_conv_silu(q_raw[rows], pq_ref[s], cq_ref[...]))
            ks.append(_causal_conv_silu(k_raw[rows], pk_ref[s], ck_ref[...]))
            vs.append(_causal_conv_silu(v_raw[rows], pv_ref[s], cv_ref[...]))
        qc = jnp.concatenate(qs, axis=0)
        kc = jnp.concatenate(ks, axis=0)
        vc = jnp.concatenate(vs, axis=0)

    qn = qc * lax.rsqrt(jnp.sum(qc * qc, axis=-1, keepdims=True) + EPS) * (dk ** -0.5)
    kn = kc * lax.rsqrt(jnp.sum(kc * kc, axis=-1, keepdims=True) + EPS)
    qb = qn.astype(BF16)
    kb = kn.astype(BF16)

    gb = gb_ref[...]
    a_coef = jnp.exp(alog_ref[...][:, :1])
    xs = gb + dtb_ref[...][:, :1]
    softplus = jnp.maximum(xs, 0.0) + jnp.log(1.0 + jnp.exp(-jnp.abs(xs)))
    g_rows = -a_coef * softplus
    beta_rows = 1.0 / (1.0 + jnp.exp(-gb))
    rid = lax.broadcasted_iota(jnp.int32, gb.shape, 0)
    x8 = jnp.where(rid < 2, g_rows, beta_rows)

    ri = lax.broadcasted_iota(jnp.int32, (m, m), 0)
    ci = lax.broadcasted_iota(jnp.int32, (m, m), 1)
    log2c = int(math.log2(c))
    same = lax.shift_right_logical(ri, log2c) == lax.shift_right_logical(ci, log2c)
    tri_incl = jnp.logical_and(same, ri >= ci)
    tri_strict = jnp.logical_and(same, ri > ci)
    tri_b = jnp.where(tri_incl, 1.0, 0.0).astype(BF16)
    eye_b = jnp.where(ri == ci, 1.0, 0.0).astype(BF16)
    eye_f = jnp.where(ri == ci, 1.0, 0.0)
    n_sq = max(log2c - 1, 0)
    nw = nw_ref[...]

    units = [(g, vh) for g in range(r // m) for vh in range(2)]
    rows_of = {g: slice(g * m, (g + 1) * m) for g in range(r // m)}
    gate = {}
    for g in range(r // m):
        rows_g = rows_of[g]
        parts = _split3(x8[:, rows_g])
        cs_col = sum(_dot_nt(tri_b, p) for p in parts)
        cs_row = sum(_dot_nt(p, tri_b) for p in parts)
        x_col = sum(_dot_nt(eye_b, p) for p in parts)
        gate[g] = (cs_col, cs_row, x_col, jnp.exp(cs_col),
                   _dot_nt(kb[rows_g], kb[rows_g]), _dot_nt(qb[rows_g], kb[rows_g]))

    g_col, b_col, eg_col, dmat, t_mat, pw = {}, {}, {}, {}, {}, {}
    for u in units:
        g, vh = u
        cs_col, cs_row, x_col, exp_col, kk, _ = gate[g]
        g_col[u] = cs_col[:, vh:vh + 1]
        b_col[u] = x_col[:, 2 + vh:3 + vh]
        eg_col[u] = exp_col[:, vh:vh + 1]
        dmat[u] = jnp.exp(jnp.where(tri_incl, g_col[u] - cs_row[vh:vh + 1, :], -jnp.inf))
        pw[u] = jnp.where(tri_strict, -(kk * dmat[u]) * b_col[u], 0.0)
        t_mat[u] = eye_f + pw[u]
    for _ in range(n_sq):
        for u in units:
            pwb = pw[u].astype(BF16)
            pw[u] = _dot(pwb, pwb)
        for u in units:
            t_mat[u] = t_mat[u] + _dot(t_mat[u].astype(BF16), pw[u].astype(BF16))

    sol, q_in = {}, {}
    for u in units:
        g, vh = u
        v_h = vc[rows_of[g], vh * GDN_DV:(vh + 1) * GDN_DV]
        rhs = jnp.concatenate([v_h * b_col[u], kn[rows_of[g]] * (b_col[u] * eg_col[u])], axis=-1)
        sol[u] = _dot(t_mat[u].astype(BF16), rhs.astype(BF16))
        q_in[u] = qn[rows_of[g]] * eg_col[u]

    chunks = [(u, ch) for u in units for ch in range(m // c)]
    k_dec, g_last, qp = {}, {}, {}
    for uc in chunks:
        u, ch = uc
        rows = slice(ch * c, (ch + 1) * c)
        g_last[uc] = g_col[u][(ch + 1) * c - 1:(ch + 1) * c, :]
        k_dec[uc] = kn[rows_of[u[0]]][rows] * jnp.exp(g_last[uc] - g_col[u][rows])
        if carry:
            qp[uc] = _dot_tn(k_dec[uc].astype(BF16), sol[u][rows].astype(BF16))

    v_new, o_inter = {}, {}
    if carry:
        states = [s_ref[0, vh] for vh in range(2)]
        sb = {}
        for g in range(r // m):
            for ch in range(m // c):
                for vh in range(2):
                    uc = ((g, vh), ch)
                    sb[uc] = states[vh].astype(BF16)
                    states[vh] = (states[vh] * jnp.exp(g_last[uc])
                                  - _dot(qp[uc][:, GDN_DV:].astype(BF16), sb[uc]) + qp[uc][:, :GDN_DV])
        for vh in range(2):
            s_ref[0, vh] = states[vh]
        for uc in chunks:
            u, ch = uc
            rows = slice(ch * c, (ch + 1) * c)
            both = _dot(jnp.concatenate([sol[u][rows, GDN_DV:].astype(BF16),
                                         q_in[u][rows].astype(BF16)], axis=0), sb[uc])
            v_new[uc] = sol[u][rows, :GDN_DV] - both[:c]
            o_inter[uc] = both[c:]
    else:
        for uc in chunks:
            u, ch = uc
            rows = slice(ch * c, (ch + 1) * c)
            seq = (u[0] * m + ch * c) // rows_per_seq
            state = s0_ref[seq, u[1]]
            s_op = op(state)
            v_new[uc] = sol[u][rows, :GDN_DV] - _dot(op(sol[u][rows, GDN_DV:]), s_op)
            o_inter[uc] = _dot(op(q_in[u][rows]), s_op)
            s_ref[seq, u[1]] = (state * jnp.exp(g_last[uc])
                                + _dot_tn(op(k_dec[uc]), op(v_new[uc])))

    for u in units:
        g, vh = u
        v_new_all = jnp.concatenate([v_new[(u, ch)] for ch in range(m // c)], axis=0)
        o = jnp.concatenate([o_inter[(u, ch)] for ch in range(m // c)], axis=0)
        attn = jnp.where(tri_incl, gate[g][5] * dmat[u], 0.0)
        o = o + _dot(attn.astype(BF16), v_new_all.astype(BF16))
        o = o * lax.rsqrt(jnp.mean(o * o, axis=-1, keepdims=True) + EPS) * nw
        z = z_ref[rows_of[g], vh * GDN_DV:(vh + 1) * GDN_DV]
        o_ref[rows_of[g], vh * GDN_DV:(vh + 1) * GDN_DV] = (o * _silu(z)).astype(o_ref.dtype)


def _gdn(proj, gates, row0, n_seq_total, seq_len, conv0_pad, s0, conv_w, a_log, dt_bias, norm_w,
         *, chunk, tb, bb):
    hk = GDN_K_HEADS * GDN_DK
    hv = GDN_V_HEADS * GDN_DV
    n_rows = n_seq_total * seq_len
    blk_rows = bb * tb
    n_inner = seq_len // tb
    n_outer = n_seq_total // bb
    carry = s0 is None
    assert row0 % blk_rows == 0 and seq_len % tb == 0 and tb % chunk == 0
    assert carry == (bb == 1) and (carry or (n_inner == 1 and tb == chunk))
    assert blk_rows % GDN_GROUP == 0 and GDN_GROUP % chunk == 0
    rb0 = row0 // blk_rows
    two = 2 * GDN_DV

    def row_map(b, h, n):
        return rb0 + b * n_inner + n

    alog8 = jnp.zeros((GDN_K_HEADS, SUBLANES, LANES), F32).at[:, :2, :].set(
        jnp.broadcast_to(a_log.reshape(GDN_K_HEADS, 2, 1), (GDN_K_HEADS, 2, LANES)))
    dtb8 = jnp.zeros((GDN_K_HEADS, SUBLANES, LANES), F32).at[:, :2, :].set(
        jnp.broadcast_to(dt_bias.reshape(GDN_K_HEADS, 2, 1), (GDN_K_HEADS, 2, LANES)))

    in_specs = [
        pl.BlockSpec((blk_rows, GDN_DK), lambda b, h, n: (row_map(b, h, n), h)),
        pl.BlockSpec((blk_rows, GDN_DK), lambda b, h, n: (row_map(b, h, n), GDN_K_HEADS + h)),
        pl.BlockSpec((blk_rows, two), lambda b, h, n: (row_map(b, h, n), 2 * hk // two + h)),
        pl.BlockSpec((blk_rows, two), lambda b, h, n: (row_map(b, h, n), CONV_DIM // two + h)),
        pl.BlockSpec((None, SUBLANES, blk_rows), lambda b, h, n: (h, 0, row_map(b, h, n))),
        pl.BlockSpec((None, SUBLANES, LANES), lambda b, h, n: (h, 0, 0)),
        pl.BlockSpec((None, SUBLANES, LANES), lambda b, h, n: (h, 0, 0)),
        pl.BlockSpec((CONV_W, GDN_DK), lambda b, h, n: (0, h)),
        pl.BlockSpec((CONV_W, GDN_DK), lambda b, h, n: (0, GDN_K_HEADS + h)),
        pl.BlockSpec((CONV_W, two), lambda b, h, n: (0, 2 * hk // two + h)),
        pl.BlockSpec((1, GDN_DV), lambda b, h, n: (0, 0)),
    ]
    args = [proj, proj, proj, proj, gates, alog8, dtb8, conv_w, conv_w, conv_w,
            norm_w.reshape(1, GDN_DV)]
    if not carry:
        in_specs += [
            pl.BlockSpec((bb, SUBLANES, GDN_DK), lambda b, h, n: (b, 0, h)),
            pl.BlockSpec((bb, SUBLANES, GDN_DK), lambda b, h, n: (b, 0, GDN_K_HEADS + h)),
            pl.BlockSpec((bb, SUBLANES, two), lambda b, h, n: (b, 0, 2 * hk // two + h)),
            pl.BlockSpec((bb, 2, GDN_DK, GDN_DV), lambda b, h, n: (b, h, 0, 0)),
        ]
        args += [conv0_pad, conv0_pad, conv0_pad, s0]
    out, s_fin = pl.pallas_call(
        functools.partial(_gdn_kernel, chunk=chunk, group=GDN_GROUP, n_seq=bb, rows_per_seq=tb,
                          carry=carry),
        grid=(n_outer, GDN_K_HEADS, n_inner),
        in_specs=in_specs,
        out_specs=[pl.BlockSpec((blk_rows, two), lambda b, h, n: (b * n_inner + n, h)),
                   pl.BlockSpec((bb, 2, GDN_DK, GDN_DV), lambda b, h, n: (b, h, 0, 0))],
        out_shape=[jax.ShapeDtypeStruct((n_rows, hv), BF16),
                   jax.ShapeDtypeStruct((n_seq_total, GDN_V_HEADS, GDN_DK, GDN_DV), F32)],
        scratch_shapes=[pltpu.VMEM((SUBLANES, 2 * GDN_DK + two), F32)],
        compiler_params=_params(("parallel", "parallel", "arbitrary")),
        name="gated_delta",
    )(*args)
    return out, s_fin


def _router_kernel(x_ref, w_ref, wr_ref, br_ref, h_ref, lg_ref):
    x = x_ref[...]
    ms = jnp.mean(x * x, axis=-1, keepdims=True)
    h = x * lax.rsqrt(ms + EPS) * w_ref[...]
    h_ref[...] = h
    h_hi, h_mid, _ = _split3(h)
    w = wr_ref[...]
    w_hi = w.astype(BF16)
    w_mid = (w - w_hi.astype(F32)).astype(BF16)
    lg_ref[...] = _dot(h_hi, w_hi) + _dot(h_hi, w_mid) + _dot(h_mid, w_hi) + br_ref[...]


def _router(x, norm_w, w_route, b_route):
    t, d = x.shape
    return pl.pallas_call(
        _router_kernel,
        grid=(t // ROW_TILE,),
        in_specs=[pl.BlockSpec((ROW_TILE, d), lambda i: (i, 0)),
                  pl.BlockSpec((1, d), lambda i: (0, 0)),
                  pl.BlockSpec((d, LANES), lambda i: (0, 0)),
                  pl.BlockSpec((1, LANES), lambda i: (0, 0))],
        out_specs=[pl.BlockSpec((ROW_TILE, d), lambda i: (i, 0)),
                   pl.BlockSpec((ROW_TILE, LANES), lambda i: (i, 0))],
        out_shape=[jax.ShapeDtypeStruct((t, d), F32),
                   jax.ShapeDtypeStruct((t, LANES), F32)],
        compiler_params=_params(("parallel",)),
        name="moe_router",
    )(x, norm_w.reshape(1, d), w_route, b_route)


def _row_gather_start(idx_ref, base, n, src_hbm, dst_ref, sem):
    def body(r, carry):
        tok = idx_ref[base + r]
        pltpu.make_async_copy(src_hbm.at[pl.ds(tok, 1)], dst_ref.at[pl.ds(r, 1)], sem).start()
        return carry
    lax.fori_loop(0, n, body, 0, unroll=GATHER_UNROLL)


def _row_gather_wait(n, src_hbm, dst_ref, sem):
    pltpu.make_async_copy(src_hbm.at[pl.ds(0, n)], dst_ref, sem).wait()


def _moe_ffn_kernel(be_ref, first_ref, nxt_ref, ws_ref, rt_ref, na_ref, h_hbm, rw_ref, win_hbm, wout_hbm,
                    o_ref, xbuf, sem, win_f, wout_f, win_sem, wout_sem, winb, woutb, *, layer):
    i = pl.program_id(0)
    n_active = na_ref[0]
    slot = lax.rem(i, 2)

    def weight_copies(e, ws):
        return (pltpu.make_async_copy(win_hbm.at[layer, e], win_f.at[ws], win_sem.at[ws]),
                pltpu.make_async_copy(wout_hbm.at[layer, e], wout_f.at[ws], wout_sem.at[ws]))

    @pl.when(jnp.logical_and(i == 0, n_active > 0))
    def _():
        for cp in weight_copies(be_ref[0], ws_ref[0]):
            cp.start()
        _row_gather_start(rt_ref, 0, MOE_BLOCK, h_hbm, xbuf.at[0], sem.at[0])

    @pl.when(i + 1 < n_active)
    def _():
        _row_gather_start(rt_ref, (i + 1) * MOE_BLOCK, MOE_BLOCK, h_hbm,
                          xbuf.at[1 - slot], sem.at[1 - slot])

    @pl.when(i < n_active)
    def _():
        ws = ws_ref[i]

        @pl.when(first_ref[i] == 1)
        def _():
            for cp in weight_copies(be_ref[i], ws):
                cp.wait()
            winb[...] = win_f[ws].astype(BF16)
            woutb[...] = wout_f[ws].astype(BF16)

            @pl.when(nxt_ref[i] >= 0)
            def _():
                for cp in weight_copies(nxt_ref[i], 1 - ws):
                    cp.start()

        _row_gather_wait(MOE_BLOCK, h_hbm, xbuf.at[slot], sem.at[slot])
        x = xbuf[slot].astype(BF16)
        mid = _dot(x, winb[...])
        act = (_silu(mid[:, :D_EXPERT]) * mid[:, D_EXPERT:]).astype(BF16)
        o_ref[...] = _dot(act, woutb[...]) * rw_ref[...]

    @pl.when(i >= n_active)
    def _():
        o_ref[...] = jnp.zeros(o_ref.shape, F32)


def _moe_ffn(h, block_e, block_first, block_next, block_wslot, row_tok, n_active, row_w,
             w_in, w_out, layer):
    t, d = h.shape
    n_blocks = block_e.shape[0]
    n_rows = n_blocks * MOE_BLOCK
    grid_spec = pltpu.PrefetchScalarGridSpec(
        num_scalar_prefetch=6,
        grid=(n_blocks,),
        in_specs=[
            pl.BlockSpec(memory_space=pl.ANY),
            pl.BlockSpec((MOE_BLOCK, 1), lambda i, *_: (i, 0)),
            pl.BlockSpec(memory_space=pl.ANY),
            pl.BlockSpec(memory_space=pl.ANY),
        ],
        out_specs=pl.BlockSpec((MOE_BLOCK, d), lambda i, *_: (i, 0)),
        scratch_shapes=[pltpu.VMEM((2, MOE_BLOCK, d), F32),
                        pltpu.SemaphoreType.DMA((2,)),
                        pltpu.VMEM((2, d, 2 * D_EXPERT), F32),
                        pltpu.VMEM((2, D_EXPERT, d), F32),
                        pltpu.SemaphoreType.DMA((2,)),
                        pltpu.SemaphoreType.DMA((2,)),
                        pltpu.VMEM((d, 2 * D_EXPERT), BF16),
                        pltpu.VMEM((D_EXPERT, d), BF16)],
    )
    return pl.pallas_call(
        functools.partial(_moe_ffn_kernel, layer=layer),
        grid_spec=grid_spec,
        out_shape=jax.ShapeDtypeStruct((n_rows, d), F32),
        compiler_params=_params(("arbitrary",)),
        name="moe_ffn",
    )(block_e, block_first, block_next, block_wslot, row_tok, n_active, h,
      row_w.reshape(n_rows, 1), w_in, w_out)


def _combine_kernel(pos_ref, ys_hbm, x_ref, o_ref, buf, sem):
    i = pl.program_id(0)
    n = pl.num_programs(0)
    slot = lax.rem(i, 2)
    rows = 2 * COMBINE_TM

    @pl.when(i == 0)
    def _():
        _row_gather_start(pos_ref, 0, rows, ys_hbm, buf.at[0], sem.at[0])

    @pl.when(i + 1 < n)
    def _():
        _row_gather_start(pos_ref, (i + 1) * rows, rows, ys_hbm, buf.at[1 - slot], sem.at[1 - slot])

    _row_gather_wait(rows, ys_hbm, buf.at[slot], sem.at[slot])
    o_ref[...] = x_ref[...] + buf[slot, :COMBINE_TM, :] + buf[slot, COMBINE_TM:, :]


def _combine(x, ys, pos):
    t, d = x.shape
    grid_spec = pltpu.PrefetchScalarGridSpec(
        num_scalar_prefetch=1,
        grid=(t // COMBINE_TM,),
        in_specs=[pl.BlockSpec(memory_space=pl.ANY),
                  pl.BlockSpec((COMBINE_TM, d), lambda i, p: (i, 0))],
        out_specs=pl.BlockSpec((COMBINE_TM, d), lambda i, p: (i, 0)),
        scratch_shapes=[pltpu.VMEM((2, 2 * COMBINE_TM, d), F32),
                        pltpu.SemaphoreType.DMA((2,))],
    )
    return pl.pallas_call(
        _combine_kernel,
        grid_spec=grid_spec,
        out_shape=jax.ShapeDtypeStruct((t, d), F32),
        compiler_params=_params(("arbitrary",)),
        name="moe_combine",
    )(pos, ys, x)


def _hier_moe(x, norm_w, w_group, b_group, w_expert, b_expert, w_in, w_out, layer):
    t, d = x.shape
    n_route = MOE_GROUPS + N_EXPERTS
    w_route = jnp.zeros((d, LANES), F32).at[:, :MOE_GROUPS].set(w_group).at[:, MOE_GROUPS:n_route].set(w_expert)
    b_route = jnp.zeros((1, LANES), F32).at[0, :MOE_GROUPS].set(b_group).at[0, MOE_GROUPS:n_route].set(b_expert)
    h, logits = _router(x, norm_w, w_route, b_route)

    g_logits = logits[:, :MOE_GROUPS]
    g_sel = jnp.argmax(g_logits, axis=-1)
    g_w = jnp.take_along_axis(jax.nn.softmax(g_logits, axis=-1), g_sel[:, None], axis=1)
    e_logits = logits[:, MOE_GROUPS:n_route].reshape(t, MOE_GROUPS, EXPERTS_PER_GROUP)
    e_logits = jnp.take_along_axis(e_logits, g_sel[:, None, None], axis=1)[:, 0]
    top_v, top_i = lax.top_k(e_logits, TOP_K)
    weights = g_w * jax.nn.softmax(top_v, axis=-1)
    expert_id = (g_sel[:, None] * EXPERTS_PER_GROUP + top_i).astype(jnp.int32)

    n_assign = t * TOP_K
    flat_e = expert_id.reshape(n_assign)
    iota = jnp.arange(n_assign, dtype=jnp.int32)
    experts = jnp.arange(N_EXPERTS, dtype=jnp.int32)
    sorted_e, order, w_sorted = lax.sort((flat_e, iota, weights.reshape(n_assign)), num_keys=1,
                                         is_stable=True)
    counts = jnp.sum((flat_e[:, None] == experts[None, :]).astype(jnp.int32), axis=0)
    start = jnp.cumsum(counts) - counts
    nblk_e = (counts + MOE_BLOCK - 1) // MOE_BLOCK
    blk_end = jnp.cumsum(nblk_e)
    blk_start = blk_end - nblk_e

    def lookup(idx, table):
        return jnp.sum(jnp.where(idx[:, None] == experts[None, :], table[None, :], 0), axis=1)

    dest_sorted = iota + lookup(sorted_e, blk_start * MOE_BLOCK - start)
    _, dest = lax.sort((order, dest_sorted), num_keys=1)

    n_blocks = -(-n_assign // MOE_BLOCK) + N_EXPERTS
    n_active = blk_end[-1].astype(jnp.int32)
    blk = jnp.minimum(jnp.arange(n_blocks, dtype=jnp.int32), jnp.maximum(n_active - 1, 0))
    block_e = jnp.minimum(jnp.sum((blk[:, None] >= blk_end[None, :]).astype(jnp.int32), axis=1),
                          N_EXPERTS - 1)
    block_first = (blk == lookup(block_e, blk_start)).astype(jnp.int32)
    active = counts > 0
    ordinal = jnp.cumsum(active.astype(jnp.int32)) - 1
    block_wslot = lookup(block_e, ordinal) % 2
    later = lax.cummin(jnp.where(active, experts, N_EXPERTS), reverse=True)
    next_e = jnp.concatenate([later[1:], jnp.full((1,), N_EXPERTS, jnp.int32)])
    block_next = lookup(block_e, jnp.where(next_e >= N_EXPERTS, -1, next_e))

    j_base = lookup(block_e, start) + (blk - lookup(block_e, blk_start)) * MOE_BLOCK
    j_end = lookup(block_e, start + counts)
    j = j_base[:, None] + jnp.arange(MOE_BLOCK, dtype=jnp.int32)[None, :]
    valid = jnp.logical_and(j < j_end[:, None],
                            jnp.arange(n_blocks, dtype=jnp.int32)[:, None] < n_active)
    j = jnp.clip(j, 0, n_assign - 1).reshape(-1)
    valid = valid.reshape(-1)
    row_tok = jnp.where(valid, order[j] // TOP_K, 0)
    row_w = jnp.where(valid, w_sorted[j], 0.0)

    ys = _moe_ffn(h, block_e, block_first, block_next, block_wslot, row_tok, n_active.reshape(1),
                  row_w, w_in, w_out, layer)
    pos = dest.reshape(t // COMBINE_TM, COMBINE_TM, TOP_K).transpose(0, 2, 1).reshape(n_assign)
    return _combine(x, ys, pos)


def kernel(x_prompt, x_sample, state_ret, state_gdn, state_conv, norm_mix, norm_ffn, norm_final,
           ret_w_in, ret_gn, ret_w_out, gdn_w_in, gdn_conv, gdn_a_log, gdn_dt_bias, gdn_norm,
           gdn_w_out, moe_w_group, moe_b_group, moe_w_expert, moe_b_expert, moe_w_in, moe_w_out):
    bp, lp, d = x_prompt.shape
    bs, ls, _ = x_sample.shape
    tp, ts = bp * lp, bs * ls
    x = jnp.concatenate([x_prompt.reshape(tp, d), x_sample.reshape(ts, d)], axis=0)

    h = _rmsnorm(x, norm_mix[0], BF16)
    ret_in = ret_w_in.shape[-1]
    proj = _matmul(h, ret_w_in.reshape(d, ret_in), ret_in, MM_TM, MM_TN)
    o_p, ret_p = _retention(proj, 0, bp, lp, 0, None, ret_gn[0],
                            chunk=RET_CHUNK, tb=RET_TB, bb=1)
    o_s, ret_s = _retention(proj, tp, bs, ls, PAST_LEN, state_ret[0], ret_gn[0],
                            chunk=ls, tb=ls, bb=RET_SAMPLE_BB)
    o = jnp.concatenate([o_p, o_s], axis=0)
    x = _matmul(o, ret_w_out.reshape(o.shape[1], d), d, ROW_TILE, MM_TN_OUT, res=x)
    x = _hier_moe(x, norm_ffn[0], moe_w_group[0], moe_b_group[0], moe_w_expert[0], moe_b_expert[0],
                  moe_w_in, moe_w_out, 0)

    h = _rmsnorm(x, norm_mix[1], BF16)
    hv = GDN_V_HEADS * GDN_DV
    n_main = CONV_DIM + hv
    w_gdn = gdn_w_in.reshape(d, gdn_w_in.shape[-1])
    proj = _matmul(h, w_gdn, n_main, MM_TM, MM_TN)
    w_tail = jnp.zeros((d, LANES), F32).at[:, :2 * GDN_V_HEADS].set(w_gdn[:, n_main:])
    tail = _matmul(h, w_tail, LANES, MM_TM, LANES)
    t = tp + ts
    bt = tail[:, :GDN_V_HEADS].reshape(t, GDN_K_HEADS, 2)
    a = tail[:, GDN_V_HEADS:2 * GDN_V_HEADS].reshape(t, GDN_K_HEADS, 2)
    gates = jnp.concatenate([a, bt, jnp.zeros((t, GDN_K_HEADS, 4), F32)], axis=-1).transpose(1, 2, 0)
    conv0_pad = jnp.pad(state_conv[0], ((0, 0), (SUBLANES - (CONV_W - 1), 0), (0, 0)))
    o_p, gdn_p = _gdn(proj, gates, 0, bp, lp, None, None, gdn_conv[0], gdn_a_log[0], gdn_dt_bias[0],
                      gdn_norm[0], chunk=GDN_CHUNK, tb=GDN_TB, bb=1)
    o_s, gdn_s = _gdn(proj, gates, tp, bs, ls, conv0_pad, state_gdn[0], gdn_conv[0], gdn_a_log[0],
                      gdn_dt_bias[0], gdn_norm[0], chunk=ls, tb=ls, bb=GDN_SAMPLE_BB)
    proj8 = proj.reshape(t // SUBLANES, SUBLANES, n_main)
    keep = SUBLANES - (CONV_W - 1)
    conv_p = jnp.concatenate(
        [lax.slice(proj8, ((b * lp + lp) // SUBLANES - 1, keep, 0),
                   ((b * lp + lp) // SUBLANES, SUBLANES, CONV_DIM)) for b in range(bp)], axis=0)
    conv_s = lax.slice(proj8, ((tp + ls) // SUBLANES - 1, keep, 0), (t // SUBLANES, SUBLANES, CONV_DIM),
                       (ls // SUBLANES, 1, 1))
    o = jnp.concatenate([o_p, o_s], axis=0)
    x = _matmul(o, gdn_w_out.reshape(hv, d), d, ROW_TILE, MM_TN_OUT, res=x)
    x = _hier_moe(x, norm_ffn[1], moe_w_group[1], moe_b_group[1], moe_w_expert[1], moe_b_expert[1],
                  moe_w_in, moe_w_out, 1)

    y = _rmsnorm(x, norm_final, F32)
    return (y[:tp].reshape(bp, lp, d), y[tp:].reshape(bs, ls, d),
            ret_p[None], ret_s[None], gdn_p[None], gdn_s[None], conv_p[None], conv_s[None])
```

```python
import functools
import math

import numpy as np
import jax
import jax.numpy as jnp
from jax import lax
from jax.experimental import pallas as pl
from jax.experimental.pallas import tpu as pltpu

F32 = jnp.float32
BF16 = jnp.bfloat16

D_MODEL = 2048
PAST_LEN = 16384
RET_HEADS = 8
RET_DK = 256
RET_DV = 512
ROPE_BASE = 10000.0
GDN_K_HEADS = 16
GDN_V_HEADS = 32
GDN_DK = 128
GDN_DV = 128
CONV_W = 4
CONV_DIM = 2 * GDN_K_HEADS * GDN_DK + GDN_V_HEADS * GDN_DV
MOE_GROUPS = 8
EXPERTS_PER_GROUP = 8
N_EXPERTS = MOE_GROUPS * EXPERTS_PER_GROUP
TOP_K = 2
D_EXPERT = D_MODEL // 4
EPS = 1e-6

LANES = 128
SUBLANES = 8
BF16_ROWS = 16
VMEM_LIMIT_BYTES = 56 * 1024 * 1024

ROW_TILE = 512
MM_TM = 1024
MM_TN = 1024
MM_TN_OUT = 512
RET_CHUNK = 128
RET_TB = 512
RET_SAMPLE_BB = 8
GDN_CHUNK = 64
GDN_TB = 1024
GDN_GROUP = 128
GDN_SAMPLE_BB = 32
MOE_BLOCK = 128
COMBINE_TM = 128
GATHER_UNROLL = 8
MOE_W_SLABS = 4


def _params(sem):
    return pltpu.CompilerParams(dimension_semantics=sem, vmem_limit_bytes=VMEM_LIMIT_BYTES)


def _dot(a, b):
    return jnp.dot(a, b, preferred_element_type=F32)


def _dot_nt(a, b):
    return lax.dot_general(a, b, (((1,), (1,)), ((), ())), preferred_element_type=F32)


def _dot_tn(a, b):
    return lax.dot_general(a, b, (((0,), (0,)), ((), ())), preferred_element_type=F32)


def _split3(x):
    hi = x.astype(BF16)
    r1 = x - hi.astype(F32)
    mid = r1.astype(BF16)
    lo = (r1 - mid.astype(F32)).astype(BF16)
    return hi, mid, lo


def _silu(x):
    return x * (1.0 / (1.0 + jnp.exp(-x)))


def _row_segments(operands, tile):
    stacks = [op if isinstance(op, (tuple, list)) else (op,) for op in operands]
    cuts = set()
    for parts in stacks:
        edge = 0
        for p in parts:
            assert p.shape[0] % tile == 0
            cuts.add(edge)
            edge += p.shape[0] // tile
        cuts.add(edge)
    cuts = sorted(cuts)
    segments = [(a, b - a) for a, b in zip(cuts[:-1], cuts[1:])]
    sources = []
    for parts in stacks:
        assert sum(p.shape[0] for p in parts) == cuts[-1] * tile
        src, edge = [], 0
        for p in parts:
            for first, _ in segments:
                if edge <= first < edge + p.shape[0] // tile:
                    src.append((p, first - edge))
            edge += p.shape[0] // tile
        sources.append(src)
    return segments, sources


def _segment_spec(block, segment, source_tile0, row_axis, col_of):
    first, n_tiles = segment

    def index_map(*ids):
        return (source_tile0 + jnp.clip(ids[row_axis] - first, 0, n_tiles - 1), col_of(*ids))
    return pl.BlockSpec(block, index_map)


def _for_current_segment(segments, tile_id, fn):
    if len(segments) == 1:
        fn(0)
        return
    for k, (first, n_tiles) in enumerate(segments):
        pl.when(jnp.logical_and(tile_id >= first, tile_id < first + n_tiles))(functools.partial(fn, k))


def _rmsnorm_kernel(*refs, segments):
    x_refs, (w_ref, o_ref) = refs[:len(segments)], refs[len(segments):]

    def run(k):
        x = x_refs[k][...]
        ms = jnp.mean(x * x, axis=-1, keepdims=True)
        o_ref[...] = (x * lax.rsqrt(ms + EPS) * w_ref[...]).astype(o_ref.dtype)

    _for_current_segment(segments, pl.program_id(0), run)


def _rmsnorm(x, w, out_dtype):
    segments, (src,) = _row_segments([x], ROW_TILE)
    d = src[0][0].shape[1]
    n_tiles = segments[-1][0] + segments[-1][1]
    return pl.pallas_call(
        functools.partial(_rmsnorm_kernel, segments=segments),
        grid=(n_tiles,),
        in_specs=[_segment_spec((ROW_TILE, d), seg, t0, 0, lambda i: 0)
                  for seg, (_, t0) in zip(segments, src)]
        + [pl.BlockSpec((1, d), lambda i: (0, 0))],
        out_specs=pl.BlockSpec((ROW_TILE, d), lambda i: (i, 0)),
        out_shape=jax.ShapeDtypeStruct((n_tiles * ROW_TILE, d), out_dtype),
        compiler_params=_params(("arbitrary",)),
        name="rmsnorm",
    )(*[p for p, _ in src], w.reshape(1, d))


def _rmsnorm_rows(x, w, out_dtype, row0, n_rows):
    d = x.shape[1]
    assert row0 % ROW_TILE == 0 and n_rows % ROW_TILE == 0
    seg = (0, n_rows // ROW_TILE)
    return pl.pallas_call(
        functools.partial(_rmsnorm_kernel, segments=[seg]),
        grid=(n_rows // ROW_TILE,),
        in_specs=[_segment_spec((ROW_TILE, d), seg, row0 // ROW_TILE, 0, lambda i: 0),
                  pl.BlockSpec((1, d), lambda i: (0, 0))],
        out_specs=pl.BlockSpec((ROW_TILE, d), lambda i: (i, 0)),
        out_shape=jax.ShapeDtypeStruct((n_rows, d), out_dtype),
        compiler_params=_params(("arbitrary",)),
        name="rmsnorm",
    )(x, w.reshape(1, d))


def _matmul_kernel(*refs, segments, has_res):
    n = len(segments)
    a_refs, w_ref = refs[:n], refs[n]
    r_refs = refs[n + 1:2 * n + 1] if has_res else None
    o_ref, wb_ref = refs[-2:]
    i = pl.program_id(1)

    @pl.when(i == 0)
    def _():
        wb_ref[...] = w_ref[...].astype(BF16)

    def run(k):
        acc = _dot(a_refs[k][...], wb_ref[...])
        o_ref[...] = acc + r_refs[k][...] if has_res else acc

    _for_current_segment(segments, i, run)


def _matmul(a, w, n_cols, tm, tn, res=None):
    segments, sources = _row_segments([a] if res is None else [a, res], tm)
    k = sources[0][0][0].shape[1]
    m_tiles = segments[-1][0] + segments[-1][1]
    in_specs = [_segment_spec((tm, k), seg, t0, 1, lambda j, i: 0)
                for seg, (_, t0) in zip(segments, sources[0])]
    in_specs.append(pl.BlockSpec((k, tn), lambda j, i: (0, j)))
    args = [p for p, _ in sources[0]] + [w]
    if res is not None:
        in_specs += [_segment_spec((tm, tn), seg, t0, 1, lambda j, i: j)
                     for seg, (_, t0) in zip(segments, sources[1])]
        args += [p for p, _ in sources[1]]
    return pl.pallas_call(
        functools.partial(_matmul_kernel, segments=segments, has_res=res is not None),
        grid=(n_cols // tn, m_tiles),
        in_specs=in_specs,
        out_specs=pl.BlockSpec((tm, tn), lambda j, i: (i, j)),
        out_shape=jax.ShapeDtypeStruct((m_tiles * tm, n_cols), F32),
        scratch_shapes=[pltpu.VMEM((k, tn), BF16)],
        compiler_params=_params(("arbitrary", "arbitrary")),
        name="matmul",
    )(*args)


def _ret_kernel(lg_ref, q_ref, k_ref, v_ref, g_ref, cos_ref, sin_ref, gn_ref, *rest,
                chunk, n_seq, chunks_per_seq, has_s0):
    if has_s0:
        s0a_ref, s0b_ref, o_ref, s_ref = rest
    else:
        o_ref, s_ref = rest
    c = chunk
    h = pl.program_id(1)
    lg = lg_ref[h]

    if not has_s0:
        @pl.when(pl.program_id(2) == 0)
        def _():
            s_ref[...] = jnp.zeros(s_ref.shape, F32)

    ii = lax.broadcasted_iota(jnp.int32, (c, 1), 0).astype(F32)
    jj = lax.broadcasted_iota(jnp.int32, (1, c), 1).astype(F32)
    diff = ii - jj
    causal = diff >= 0.0
    decay = jnp.where(causal, jnp.exp(lg * jnp.where(causal, diff, 0.0)), 0.0)
    q_decay = jnp.exp(lg * (ii + 1.0))
    k_decay = jnp.exp(lg * (c - 1.0 - ii))
    chunk_decay = jnp.exp(jnp.full((1, 1), lg * c, F32))
    half = RET_DK // 2
    k_scale = RET_DK ** -0.5
    gn_w = gn_ref[...]
    op = (lambda t: t.astype(BF16)) if c % BF16_ROWS == 0 else (lambda t: t)
    outs = []

    for s in range(n_seq):
        state = jnp.concatenate([s0a_ref[s], s0b_ref[s]], axis=0) if has_s0 else s_ref[s]
        for ci in range(chunks_per_seq):
            r0 = (s * chunks_per_seq + ci) * c
            rows = slice(r0, r0 + c)
            cos = cos_ref[rows, :]
            sin = sin_ref[rows, :]
            q = q_ref[rows, :]
            k = k_ref[rows, :]
            q1, q2 = q[:, :half], q[:, half:]
            k1, k2 = k[:, :half], k[:, half:]
            qr = jnp.concatenate([q1 * cos - q2 * sin, q1 * sin + q2 * cos], axis=-1)
            kr = jnp.concatenate([k1 * cos - k2 * sin, k1 * sin + k2 * cos], axis=-1) * k_scale
            vb = op(v_ref[rows, :])
            qb = op(qr)
            scores = _dot_nt(qb, op(kr)) * decay
            inner = _dot(op(scores), vb)
            cross = _dot(qb, op(state)) * q_decay
            o = inner + cross
            state = state * chunk_decay + _dot_tn(op(kr * k_decay), vb)
            mu = jnp.mean(o, axis=-1, keepdims=True)
            oc = o - mu
            var = jnp.mean(oc * oc, axis=-1, keepdims=True)
            on = oc * lax.rsqrt(var + EPS) * gn_w
            res = on * _silu(g_ref[rows, :])
            if c % BF16_ROWS == 0:
                o_ref[rows, :] = res.astype(o_ref.dtype)
            else:
                outs.append(res)
        s_ref[s] = state
    if outs:
        o_ref[...] = jnp.concatenate(outs, axis=0).astype(o_ref.dtype)


def _retention(proj, row0, n_seq_total, seq_len, pos0, s0, gn_w, *, chunk, tb, bb):
    hq = RET_HEADS * RET_DK
    n_rows = n_seq_total * seq_len
    blk_rows = bb * tb
    n_inner = seq_len // tb
    n_outer = n_seq_total // bb
    assert row0 % blk_rows == 0 and seq_len % tb == 0 and tb % chunk == 0
    assert s0 is None or n_inner == 1
    rb0 = row0 // blk_rows
    half = RET_DK // 2

    pos = (pos0 + jnp.arange(seq_len)).astype(F32)
    inv = ROPE_BASE ** (-jnp.arange(half, dtype=F32) / half)
    ang = pos[:, None] * inv[None, :]
    cos = jnp.tile(jnp.cos(ang), (bb, 1)) if n_inner == 1 else jnp.cos(ang)
    sin = jnp.tile(jnp.sin(ang), (bb, 1)) if n_inner == 1 else jnp.sin(ang)
    log_gamma = jnp.log1p(-jnp.exp2(-5.0 - jnp.arange(RET_HEADS, dtype=F32)))

    def row_map(b, h, n):
        return rb0 + b * n_inner + n

    in_specs = [
        pl.BlockSpec(memory_space=pltpu.SMEM),
        pl.BlockSpec((blk_rows, RET_DK), lambda b, h, n: (row_map(b, h, n), h)),
        pl.BlockSpec((blk_rows, RET_DK), lambda b, h, n: (row_map(b, h, n), RET_HEADS + h)),
        pl.BlockSpec((blk_rows, RET_DV), lambda b, h, n: (row_map(b, h, n), 2 * hq // RET_DV + h)),
        pl.BlockSpec((blk_rows, RET_DV), lambda b, h, n: (row_map(b, h, n), 2 * hq // RET_DV + RET_HEADS + h)),
        pl.BlockSpec((blk_rows, half), lambda b, h, n: (n, 0)),
        pl.BlockSpec((blk_rows, half), lambda b, h, n: (n, 0)),
        pl.BlockSpec((None, 1, RET_DV), lambda b, h, n: (h, 0, 0)),
    ]
    args = [log_gamma, proj, proj, proj, proj, cos, sin, gn_w.reshape(RET_HEADS, 1, RET_DV)]
    if s0 is not None:
        in_specs += [pl.BlockSpec((bb, None, RET_DK // 2, RET_DV), lambda b, h, n: (b, h, 0, 0)),
                     pl.BlockSpec((bb, None, RET_DK // 2, RET_DV), lambda b, h, n: (b, h, 1, 0))]
        args += [s0, s0]
    out, s_fin = pl.pallas_call(
        functools.partial(_ret_kernel, chunk=chunk, n_seq=bb, chunks_per_seq=tb // chunk,
                          has_s0=s0 is not None),
        grid=(n_outer, RET_HEADS, n_inner),
        in_specs=in_specs,
        out_specs=[pl.BlockSpec((blk_rows, RET_DV), lambda b, h, n: (b * n_inner + n, h)),
                   pl.BlockSpec((bb, None, RET_DK, RET_DV), lambda b, h, n: (b, h, 0, 0))],
        out_shape=[jax.ShapeDtypeStruct((n_rows, RET_HEADS * RET_DV), BF16),
                   jax.ShapeDtypeStruct((n_seq_total, RET_HEADS, RET_DK, RET_DV), F32)],
        compiler_params=_params(("parallel", "parallel", "arbitrary")),
        name="retention",
    )(*args)
    return out, s_fin


def _shift_rows(x, prev8, s):
    rolled = pltpu.roll(x, s, 0)
    head = pltpu.roll(prev8, s, 0)
    row = lax.broadcasted_iota(jnp.int32, (SUBLANES, x.shape[1]), 0)
    top = jnp.where(row < s, head, rolled[:SUBLANES])
    if x.shape[0] == SUBLANES:
        return top
    return jnp.concatenate([top, rolled[SUBLANES:]], axis=0)


def _causal_conv_silu(x, prev8, cw):
    acc = x * cw[CONV_W - 1:CONV_W, :]
    for s in range(1, CONV_W):
        acc = acc + _shift_rows(x, prev8, s) * cw[CONV_W - 1 - s:CONV_W - s, :]
    return _silu(acc)


def _gdn_kernel(q_ref, k_ref, v_ref, z_ref, gb_ref, alog_ref, dtb_ref, cq_ref, ck_ref, cv_ref,
                nw_ref, *rest, chunk, group, n_seq, rows_per_seq, carry):
    has_init = not carry
    if has_init:
        pq_ref, pk_ref, pv_ref, s0_ref, o_ref, s_ref, prev_ref = rest
    else:
        o_ref, s_ref, prev_ref = rest
    c = chunk
    r = n_seq * rows_per_seq
    m = group
    dk = GDN_DK
    op = (lambda t: t.astype(BF16)) if c % BF16_ROWS == 0 else (lambda t: t)

    if carry:
        @pl.when(pl.program_id(2) == 0)
        def _():
            s_ref[...] = jnp.zeros(s_ref.shape, F32)
            prev_ref[...] = jnp.zeros(prev_ref.shape, F32)

    q_raw, k_raw, v_raw = q_ref[...], k_ref[...], v_ref[...]
    if carry:
        prev = prev_ref[...]
        qc = _causal_conv_silu(q_raw, prev[:, :dk], cq_ref[...])
        kc = _causal_conv_silu(k_raw, prev[:, dk:2 * dk], ck_ref[...])
        vc = _causal_conv_silu(v_raw, prev[:, 2 * dk:], cv_ref[...])
        prev_ref[...] = jnp.concatenate(
            [q_raw[r - SUBLANES:], k_raw[r - SUBLANES:], v_raw[r - SUBLANES:]], axis=-1)
    else:
        qs, ks, vs = [], [], []
        for s in range(n_seq):
            rows = slice(s * rows_per_seq, (s + 1) * rows_per_seq)
            qs.append(_causal_conv_silu(q_raw[rows], pq_ref[s], cq_ref[...]))
            ks.append(_causal_conv_silu(k_raw[rows], pk_ref[s], ck_ref[...]))
            vs.append(_causal_conv_silu(v_raw[rows], pv_ref[s], cv_ref[...]))
        qc = jnp.concatenate(qs, axis=0)
        kc = jnp.concatenate(ks, axis=0)
        vc = jnp.concatenate(vs, axis=0)

    qn = qc * lax.rsqrt(jnp.sum(qc * qc, axis=-1, keepdims=True) + EPS) * (dk ** -0.5)
    kn = kc * lax.rsqrt(jnp.sum(kc * kc, axis=-1, keepdims=True) + EPS)
    qb = qn.astype(BF16)
    kb = kn.astype(BF16)

    gb = gb_ref[...]
    a_coef = jnp.exp(alog_ref[...][:, :1])
    xs = gb + dtb_ref[...][:, :1]
    softplus = jnp.maximum(xs, 0.0) + jnp.log(1.0 + jnp.exp(-jnp.abs(xs)))
    g_rows = -a_coef * softplus
    beta_rows = 1.0 / (1.0 + jnp.exp(-gb))
    rid = lax.broadcasted_iota(jnp.int32, gb.shape, 0)
    x8 = jnp.where(rid < 2, g_rows, beta_rows)

    ri = lax.broadcasted_iota(jnp.int32, (m, m), 0)
    ci = lax.broadcasted_iota(jnp.int32, (m, m), 1)
    log2c = int(math.log2(c))
    same = lax.shift_right_logical(ri, log2c) == lax.shift_right_logical(ci, log2c)
    tri_incl = jnp.logical_and(same, ri >= ci)
    tri_strict = jnp.logical_and(same, ri > ci)
    tri_b = jnp.where(tri_incl, 1.0, 0.0).astype(BF16)
    eye_b = jnp.where(ri == ci, 1.0, 0.0).astype(BF16)
    eye_f = jnp.where(ri == ci, 1.0, 0.0)
    n_sq = max(log2c - 1, 0)
    nw = nw_ref[...]

    units = [(g, vh) for g in range(r // m) for vh in range(2)]
    rows_of = {g: slice(g * m, (g + 1) * m) for g in range(r // m)}
    gate = {}
    for g in range(r // m):
        rows_g = rows_of[g]
        parts = _split3(x8[:, rows_g])
        cs_col = sum(_dot_nt(tri_b, p) for p in parts)
        cs_row = sum(_dot_nt(p, tri_b) for p in parts)
        x_col = sum(_dot_nt(eye_b, p) for p in parts)
        gate[g] = (cs_col, cs_row, x_col, jnp.exp(cs_col),
                   _dot_nt(kb[rows_g], kb[rows_g]), _dot_nt(qb[rows_g], kb[rows_g]))

    g_col, b_col, eg_col, dmat, t_mat, pw = {}, {}, {}, {}, {}, {}
    for u in units:
        g, vh = u
        cs_col, cs_row, x_col, exp_col, kk, _ = gate[g]
        g_col[u] = cs_col[:, vh:vh + 1]
        b_col[u] = x_col[:, 2 + vh:3 + vh]
        eg_col[u] = exp_col[:, vh:vh + 1]
        dmat[u] = jnp.exp(jnp.where(tri_incl, g_col[u] - cs_row[vh:vh + 1, :], -jnp.inf))
        pw[u] = jnp.where(tri_strict, -(kk * dmat[u]) * b_col[u], 0.0)
        t_mat[u] = eye_f + pw[u]
    for _ in range(n_sq):
        for u in units:
            pwb = pw[u].astype(BF16)
            pw[u] = _dot(pwb, pwb)
        for u in units:
            t_mat[u] = t_mat[u] + _dot(t_mat[u].astype(BF16), pw[u].astype(BF16))

    sol, q_in = {}, {}
    for u in units:
        g, vh = u
        v_h = vc[rows_of[g], vh * GDN_DV:(vh + 1) * GDN_DV]
        rhs = jnp.concatenate([v_h * b_col[u], kn[rows_of[g]] * (b_col[u] * eg_col[u])], axis=-1)
        sol[u] = _dot(t_mat[u].astype(BF16), rhs.astype(BF16))
        q_in[u] = qn[rows_of[g]] * eg_col[u]

    chunks = [(u, ch) for u in units for ch in range(m // c)]
    k_dec, g_last, qp = {}, {}, {}
    for uc in chunks:
        u, ch = uc
        rows = slice(ch * c, (ch + 1) * c)
        g_last[uc] = g_col[u][(ch + 1) * c - 1:(ch + 1) * c, :]
        k_dec[uc] = kn[rows_of[u[0]]][rows] * jnp.exp(g_last[uc] - g_col[u][rows])
        if carry:
            qp[uc] = _dot_tn(k_dec[uc].astype(BF16), sol[u][rows].astype(BF16))

    v_new, o_inter = {}, {}
    if carry:
        states = [s_ref[0, vh] for vh in range(2)]
        sb = {}
        for g in range(r // m):
            for ch in range(m // c):
                for vh in range(2):
                    uc = ((g, vh), ch)
                    sb[uc] = states[vh].astype(BF16)
                    states[vh] = (states[vh] * jnp.exp(g_last[uc])
                                  - _dot(qp[uc][:, GDN_DV:].astype(BF16), sb[uc]) + qp[uc][:, :GDN_DV])
        for vh in range(2):
            s_ref[0, vh] = states[vh]
        for uc in chunks:
            u, ch = uc
            rows = slice(ch * c, (ch + 1) * c)
            both = _dot(jnp.concatenate([sol[u][rows, GDN_DV:].astype(BF16),
                                         q_in[u][rows].astype(BF16)], axis=0), sb[uc])
            v_new[uc] = sol[u][rows, :GDN_DV] - both[:c]
            o_inter[uc] = both[c:]
    else:
        for uc in chunks:
            u, ch = uc
            rows = slice(ch * c, (ch + 1) * c)
            seq = (u[0] * m + ch * c) // rows_per_seq
            state = s0_ref[seq, u[1]]
            both = _dot(jnp.concatenate([sol[u][rows, GDN_DV:], q_in[u][rows]], axis=0).astype(BF16),
                        state.astype(BF16))
            v_new[uc] = sol[u][rows, :GDN_DV] - both[:c]
            o_inter[uc] = both[c:]
            s_ref[seq, u[1]] = (state * jnp.exp(g_last[uc])
                                + _dot_tn(op(k_dec[uc]), op(v_new[uc])))

    for u in units:
        g, vh = u
        v_new_all = jnp.concatenate([v_new[(u, ch)] for ch in range(m // c)], axis=0)
        o = jnp.concatenate([o_inter[(u, ch)] for ch in range(m // c)], axis=0)
        attn = jnp.where(tri_incl, gate[g][5] * dmat[u], 0.0)
        o = o + _dot(attn.astype(BF16), v_new_all.astype(BF16))
        o = o * lax.rsqrt(jnp.mean(o * o, axis=-1, keepdims=True) + EPS) * nw
        z = z_ref[rows_of[g], vh * GDN_DV:(vh + 1) * GDN_DV]
        o_ref[rows_of[g], vh * GDN_DV:(vh + 1) * GDN_DV] = (o * _silu(z)).astype(o_ref.dtype)


def _gdn(proj, gates, row0, n_seq_total, seq_len, conv0_pad, s0, conv_w, a_log, dt_bias, norm_w,
         *, chunk, tb, bb):
    hk = GDN_K_HEADS * GDN_DK
    hv = GDN_V_HEADS * GDN_DV
    n_rows = n_seq_total * seq_len
    blk_rows = bb * tb
    n_inner = seq_len // tb
    n_outer = n_seq_total // bb
    carry = s0 is None
    assert row0 % blk_rows == 0 and seq_len % tb == 0 and tb % chunk == 0
    assert carry == (bb == 1) and (carry or (n_inner == 1 and tb == chunk))
    assert blk_rows % GDN_GROUP == 0 and GDN_GROUP % chunk == 0
    rb0 = row0 // blk_rows
    two = 2 * GDN_DV

    def row_map(b, h, n):
        return rb0 + b * n_inner + n

    alog8 = jnp.zeros((GDN_K_HEADS, SUBLANES, LANES), F32).at[:, :2, :].set(
        jnp.broadcast_to(a_log.reshape(GDN_K_HEADS, 2, 1), (GDN_K_HEADS, 2, LANES)))
    dtb8 = jnp.zeros((GDN_K_HEADS, SUBLANES, LANES), F32).at[:, :2, :].set(
        jnp.broadcast_to(dt_bias.reshape(GDN_K_HEADS, 2, 1), (GDN_K_HEADS, 2, LANES)))

    in_specs = [
        pl.BlockSpec((blk_rows, GDN_DK), lambda b, h, n: (row_map(b, h, n), h)),
        pl.BlockSpec((blk_rows, GDN_DK), lambda b, h, n: (row_map(b, h, n), GDN_K_HEADS + h)),
        pl.BlockSpec((blk_rows, two), lambda b, h, n: (row_map(b, h, n), 2 * hk // two + h)),
        pl.BlockSpec((blk_rows, two), lambda b, h, n: (row_map(b, h, n), CONV_DIM // two + h)),
        pl.BlockSpec((None, SUBLANES, blk_rows), lambda b, h, n: (h, 0, row_map(b, h, n))),
        pl.BlockSpec((None, SUBLANES, LANES), lambda b, h, n: (h, 0, 0)),
        pl.BlockSpec((None, SUBLANES, LANES), lambda b, h, n: (h, 0, 0)),
        pl.BlockSpec((CONV_W, GDN_DK), lambda b, h, n: (0, h)),
        pl.BlockSpec((CONV_W, GDN_DK), lambda b, h, n: (0, GDN_K_HEADS + h)),
        pl.BlockSpec((CONV_W, two), lambda b, h, n: (0, 2 * hk // two + h)),
        pl.BlockSpec((1, GDN_DV), lambda b, h, n: (0, 0)),
    ]
    args = [proj, proj, proj, proj, gates, alog8, dtb8, conv_w, conv_w, conv_w,
            norm_w.reshape(1, GDN_DV)]
    if not carry:
        in_specs += [
            pl.BlockSpec((bb, SUBLANES, GDN_DK), lambda b, h, n: (b, 0, h)),
            pl.BlockSpec((bb, SUBLANES, GDN_DK), lambda b, h, n: (b, 0, GDN_K_HEADS + h)),
            pl.BlockSpec((bb, SUBLANES, two), lambda b, h, n: (b, 0, 2 * hk // two + h)),
            pl.BlockSpec((bb, 2, GDN_DK, GDN_DV), lambda b, h, n: (b, h, 0, 0)),
        ]
        args += [conv0_pad, conv0_pad, conv0_pad, s0]
    out, s_fin = pl.pallas_call(
        functools.partial(_gdn_kernel, chunk=chunk, group=GDN_GROUP, n_seq=bb, rows_per_seq=tb,
                          carry=carry),
        grid=(n_outer, GDN_K_HEADS, n_inner),
        in_specs=in_specs,
        out_specs=[pl.BlockSpec((blk_rows, two), lambda b, h, n: (b * n_inner + n, h)),
                   pl.BlockSpec((bb, 2, GDN_DK, GDN_DV), lambda b, h, n: (b, h, 0, 0))],
        out_shape=[jax.ShapeDtypeStruct((n_rows, hv), BF16),
                   jax.ShapeDtypeStruct((n_seq_total, GDN_V_HEADS, GDN_DK, GDN_DV), F32)],
        scratch_shapes=[pltpu.VMEM((SUBLANES, 2 * GDN_DK + two), F32)],
        compiler_params=_params(("parallel", "parallel", "arbitrary")),
        name="gated_delta",
    )(*args)
    return out, s_fin


def _router_kernel(x_ref, w_ref, wr_ref, br_ref, h_ref, lg_ref):
    x = x_ref[...]
    ms = jnp.mean(x * x, axis=-1, keepdims=True)
    h = x * lax.rsqrt(ms + EPS) * w_ref[...]
    h_ref[...] = h
    lg_ref[...] = _dot_nt(wr_ref[...].astype(BF16), h.astype(BF16)) + br_ref[...]


def _router(x, norm_w, w_route, b_route):
    t, d = x.shape
    return pl.pallas_call(
        _router_kernel,
        grid=(t // ROW_TILE,),
        in_specs=[pl.BlockSpec((ROW_TILE, d), lambda i: (i, 0)),
                  pl.BlockSpec((1, d), lambda i: (0, 0)),
                  pl.BlockSpec((LANES, d), lambda i: (0, 0)),
                  pl.BlockSpec((LANES, 1), lambda i: (0, 0))],
        out_specs=[pl.BlockSpec((ROW_TILE, d), lambda i: (i, 0)),
                   pl.BlockSpec((LANES, ROW_TILE), lambda i: (0, i))],
        out_shape=[jax.ShapeDtypeStruct((t, d), F32),
                   jax.ShapeDtypeStruct((LANES, t), F32)],
        compiler_params=_params(("parallel",)),
        name="moe_router",
    )(x, norm_w.reshape(1, d), w_route, b_route)


def _row_gather_start(idx_ref, base, n, src_hbm, dst_ref, sem):
    def body(r, carry):
        tok = idx_ref[base + r]
        pltpu.make_async_copy(src_hbm.at[pl.ds(tok, 1)], dst_ref.at[pl.ds(r, 1)], sem).start()
        return carry
    lax.fori_loop(0, n, body, 0, unroll=GATHER_UNROLL)


def _row_gather_wait(n, src_hbm, dst_ref, sem):
    pltpu.make_async_copy(src_hbm.at[pl.ds(0, n)], dst_ref, sem).wait()


def _moe_ffn_kernel(be_ref, first_ref, nxt_ref, ws_ref, rt_ref, na_ref, h_hbm, rw_ref, win_hbm, wout_hbm,
                    o_ref, xbuf, sem, win_f, wout_f, win_sem, wout_sem, winb, woutb, *, layer):
    i = pl.program_id(0)
    n_active = na_ref[0]
    slot = lax.rem(i, 2)

    def weight_copies(e, ws):
        r_in = win_f.shape[1] // MOE_W_SLABS
        r_out = wout_f.shape[1] // MOE_W_SLABS
        cps = []
        for k in range(MOE_W_SLABS):
            cps.append(pltpu.make_async_copy(win_hbm.at[layer, e, pl.ds(k * r_in, r_in)],
                                             win_f.at[ws, pl.ds(k * r_in, r_in)], win_sem.at[ws, k]))
            cps.append(pltpu.make_async_copy(wout_hbm.at[layer, e, pl.ds(k * r_out, r_out)],
                                             wout_f.at[ws, pl.ds(k * r_out, r_out)], wout_sem.at[ws, k]))
        return cps

    @pl.when(jnp.logical_and(i == 0, n_active > 0))
    def _():
        for cp in weight_copies(be_ref[0], ws_ref[0]):
            cp.start()
        _row_gather_start(rt_ref, 0, MOE_BLOCK, h_hbm, xbuf.at[0], sem.at[0])

    @pl.when(i + 1 < n_active)
    def _():
        _row_gather_start(rt_ref, (i + 1) * MOE_BLOCK, MOE_BLOCK, h_hbm,
                          xbuf.at[1 - slot], sem.at[1 - slot])

    @pl.when(i < n_active)
    def _():
        ws = ws_ref[i]

        @pl.when(first_ref[i] == 1)
        def _():
            for cp in weight_copies(be_ref[i], ws):
                cp.wait()
            winb[...] = win_f[ws].astype(BF16)
            woutb[...] = wout_f[ws].astype(BF16)

            @pl.when(nxt_ref[i] >= 0)
            def _():
                for cp in weight_copies(nxt_ref[i], 1 - ws):
                    cp.start()

        _row_gather_wait(MOE_BLOCK, h_hbm, xbuf.at[slot], sem.at[slot])
        x = xbuf[slot].astype(BF16)
        mid = _dot(x, winb[...])
        act = (_silu(mid[:, :D_EXPERT]) * mid[:, D_EXPERT:]).astype(BF16)
        o_ref[...] = _dot(act, woutb[...]) * rw_ref[...]

    @pl.when(i >= n_active)
    def _():
        o_ref[...] = jnp.zeros(o_ref.shape, F32)


def _moe_ffn(h, block_e, block_first, block_next, block_wslot, row_tok, n_active, row_w,
             w_in, w_out, layer):
    t, d = h.shape
    n_blocks = block_e.shape[0]
    n_rows = n_blocks * MOE_BLOCK
    grid_spec = pltpu.PrefetchScalarGridSpec(
        num_scalar_prefetch=6,
        grid=(n_blocks,),
        in_specs=[
            pl.BlockSpec(memory_space=pl.ANY),
            pl.BlockSpec((MOE_BLOCK, 1), lambda i, *_: (i, 0)),
            pl.BlockSpec(memory_space=pl.ANY),
            pl.BlockSpec(memory_space=pl.ANY),
        ],
        out_specs=pl.BlockSpec((MOE_BLOCK, d), lambda i, *_: (i, 0)),
        scratch_shapes=[pltpu.VMEM((2, MOE_BLOCK, d), F32),
                        pltpu.SemaphoreType.DMA((2,)),
                        pltpu.VMEM((2, d, 2 * D_EXPERT), F32),
                        pltpu.VMEM((2, D_EXPERT, d), F32),
                        pltpu.SemaphoreType.DMA((2, MOE_W_SLABS)),
                        pltpu.SemaphoreType.DMA((2, MOE_W_SLABS)),
                        pltpu.VMEM((d, 2 * D_EXPERT), BF16),
                        pltpu.VMEM((D_EXPERT, d), BF16)],
    )
    return pl.pallas_call(
        functools.partial(_moe_ffn_kernel, layer=layer),
        grid_spec=grid_spec,
        out_shape=jax.ShapeDtypeStruct((n_rows, d), F32),
        compiler_params=_params(("arbitrary",)),
        name="moe_ffn",
    )(block_e, block_first, block_next, block_wslot, row_tok, n_active, h,
      row_w.reshape(n_rows, 1), w_in, w_out)


def _combine_kernel(pos_ref, ys_hbm, x_ref, o_ref, buf, sem):
    i = pl.program_id(0)
    n = pl.num_programs(0)
    slot = lax.rem(i, 2)
    rows = 2 * COMBINE_TM

    @pl.when(i == 0)
    def _():
        _row_gather_start(pos_ref, 0, rows, ys_hbm, buf.at[0], sem.at[0])

    @pl.when(i + 1 < n)
    def _():
        _row_gather_start(pos_ref, (i + 1) * rows, rows, ys_hbm, buf.at[1 - slot], sem.at[1 - slot])

    _row_gather_wait(rows, ys_hbm, buf.at[slot], sem.at[slot])
    o_ref[...] = x_ref[...] + buf[slot, :COMBINE_TM, :] + buf[slot, COMBINE_TM:, :]


def _combine(x, ys, pos):
    t, d = x.shape
    grid_spec = pltpu.PrefetchScalarGridSpec(
        num_scalar_prefetch=1,
        grid=(t // COMBINE_TM,),
        in_specs=[pl.BlockSpec(memory_space=pl.ANY),
                  pl.BlockSpec((COMBINE_TM, d), lambda i, p: (i, 0))],
        out_specs=pl.BlockSpec((COMBINE_TM, d), lambda i, p: (i, 0)),
        scratch_shapes=[pltpu.VMEM((2, 2 * COMBINE_TM, d), F32),
                        pltpu.SemaphoreType.DMA((2,))],
    )
    return pl.pallas_call(
        _combine_kernel,
        grid_spec=grid_spec,
        out_shape=jax.ShapeDtypeStruct((t, d), F32),
        compiler_params=_params(("arbitrary",)),
        name="moe_combine",
    )(pos, ys, x)


def _hier_moe(x, norm_w, w_group, b_group, w_expert, b_expert, w_in, w_out, layer):
    t, d = x.shape
    n_route = MOE_GROUPS + N_EXPERTS
    w_route = jnp.zeros((LANES, d), F32).at[:MOE_GROUPS].set(w_group.T).at[MOE_GROUPS:n_route].set(w_expert.T)
    b_route = jnp.zeros((LANES, 1), F32).at[:MOE_GROUPS, 0].set(b_group).at[MOE_GROUPS:n_route, 0].set(b_expert)
    h, logits = _router(x, norm_w, w_route, b_route)

    g_logits = logits[:MOE_GROUPS]
    g_max = jnp.max(g_logits, axis=0)
    g_sel = jnp.argmax(g_logits, axis=0).astype(jnp.int32)
    g_w = 1.0 / jnp.sum(jnp.exp(g_logits - g_max[None, :]), axis=0)
    e_all = logits[MOE_GROUPS:n_route].reshape(MOE_GROUPS, EXPERTS_PER_GROUP, t)
    grp = jnp.arange(MOE_GROUPS, dtype=jnp.int32)[:, None, None]
    e_logits = jnp.sum(jnp.where(grp == g_sel[None, None, :], e_all, 0.0), axis=0)
    slot_id = jnp.arange(EXPERTS_PER_GROUP, dtype=jnp.int32)[:, None]
    i1 = jnp.argmax(e_logits, axis=0).astype(jnp.int32)
    v1 = jnp.max(e_logits, axis=0)
    rest = jnp.where(slot_id == i1[None, :], -jnp.inf, e_logits)
    i2 = jnp.argmax(rest, axis=0).astype(jnp.int32)
    v2 = jnp.max(rest, axis=0)
    e21 = jnp.exp(v2 - v1)
    weights = jnp.stack([g_w / (1.0 + e21), g_w * e21 / (1.0 + e21)], axis=1)
    expert_id = jnp.stack([g_sel * EXPERTS_PER_GROUP + i1, g_sel * EXPERTS_PER_GROUP + i2], axis=1)

    n_assign = t * TOP_K
    flat_e = expert_id.reshape(n_assign)
    iota = jnp.arange(n_assign, dtype=jnp.int32)
    experts = jnp.arange(N_EXPERTS, dtype=jnp.int32)
    sorted_e, order, w_sorted = lax.sort((flat_e, iota, weights.reshape(n_assign)), num_keys=1,
                                         is_stable=True)
    counts = jnp.sum((flat_e[:, None] == experts[None, :]).astype(jnp.int32), axis=0)
    start = jnp.cumsum(counts) - counts
    nblk_e = (counts + MOE_BLOCK - 1) // MOE_BLOCK
    blk_end = jnp.cumsum(nblk_e)
    blk_start = blk_end - nblk_e

    def lookup(idx, table):
        return jnp.sum(jnp.where(idx[:, None] == experts[None, :], table[None, :], 0), axis=1)

    dest_sorted = iota + lookup(sorted_e, blk_start * MOE_BLOCK - start)
    _, dest = lax.sort((order, dest_sorted), num_keys=1)

    n_blocks = -(-n_assign // MOE_BLOCK) + N_EXPERTS
    n_active = blk_end[-1].astype(jnp.int32)
    blk = jnp.minimum(jnp.arange(n_blocks, dtype=jnp.int32), jnp.maximum(n_active - 1, 0))
    block_e = jnp.minimum(jnp.sum((blk[:, None] >= blk_end[None, :]).astype(jnp.int32), axis=1),
                          N_EXPERTS - 1)
    block_first = (blk == lookup(block_e, blk_start)).astype(jnp.int32)
    active = counts > 0
    ordinal = jnp.cumsum(active.astype(jnp.int32)) - 1
    block_wslot = lookup(block_e, ordinal) % 2
    later = lax.cummin(jnp.where(active, experts, N_EXPERTS), reverse=True)
    next_e = jnp.concatenate([later[1:], jnp.full((1,), N_EXPERTS, jnp.int32)])
    block_next = lookup(block_e, jnp.where(next_e >= N_EXPERTS, -1, next_e))

    j_base = lookup(block_e, start) + (blk - lookup(block_e, blk_start)) * MOE_BLOCK
    j_end = lookup(block_e, start + counts)
    j = j_base[:, None] + jnp.arange(MOE_BLOCK, dtype=jnp.int32)[None, :]
    valid = jnp.logical_and(j < j_end[:, None],
                            jnp.arange(n_blocks, dtype=jnp.int32)[:, None] < n_active)
    j = jnp.clip(j, 0, n_assign - 1).reshape(-1)
    valid = valid.reshape(-1)
    row_tok = jnp.where(valid, order[j] // TOP_K, 0)
    row_w = jnp.where(valid, w_sorted[j], 0.0)

    ys = _moe_ffn(h, block_e, block_first, block_next, block_wslot, row_tok, n_active.reshape(1),
                  row_w, w_in, w_out, layer)
    pos = dest.reshape(t // COMBINE_TM, COMBINE_TM, TOP_K).transpose(0, 2, 1).reshape(n_assign)
    return _combine(x, ys, pos)


def kernel(x_prompt, x_sample, state_ret, state_gdn, state_conv, norm_mix, norm_ffn, norm_final,
           ret_w_in, ret_gn, ret_w_out, gdn_w_in, gdn_conv, gdn_a_log, gdn_dt_bias, gdn_norm,
           gdn_w_out, moe_w_group, moe_b_group, moe_w_expert, moe_b_expert, moe_w_in, moe_w_out):
    bp, lp, d = x_prompt.shape
    bs, ls, _ = x_sample.shape
    tp, ts = bp * lp, bs * ls
    x = (x_prompt.reshape(tp, d), x_sample.reshape(ts, d))

    h = _rmsnorm(x, norm_mix[0], BF16)
    ret_in = ret_w_in.shape[-1]
    proj = _matmul(h, ret_w_in.reshape(d, ret_in), ret_in, MM_TM, MM_TN)
    o_p, ret_p = _retention(proj, 0, bp, lp, 0, None, ret_gn[0],
                            chunk=RET_CHUNK, tb=RET_TB, bb=1)
    o_s, ret_s = _retention(proj, tp, bs, ls, PAST_LEN, state_ret[0], ret_gn[0],
                            chunk=ls, tb=ls, bb=RET_SAMPLE_BB)
    x = _matmul((o_p, o_s), ret_w_out.reshape(o_p.shape[1], d), d, ROW_TILE, MM_TN_OUT, res=x)
    x = _hier_moe(x, norm_ffn[0], moe_w_group[0], moe_b_group[0], moe_w_expert[0], moe_b_expert[0],
                  moe_w_in, moe_w_out, 0)

    h = _rmsnorm(x, norm_mix[1], BF16)
    hv = GDN_V_HEADS * GDN_DV
    n_main = CONV_DIM + hv
    w_gdn = gdn_w_in.reshape(d, gdn_w_in.shape[-1])
    proj = _matmul(h, w_gdn, n_main, MM_TM, MM_TN)
    w_tail = jnp.zeros((d, LANES), F32).at[:, :2 * GDN_V_HEADS].set(w_gdn[:, n_main:])
    tail = _matmul(h, w_tail, LANES, MM_TM, LANES)
    t = tp + ts
    bt = tail[:, :GDN_V_HEADS].reshape(t, GDN_K_HEADS, 2)
    a = tail[:, GDN_V_HEADS:2 * GDN_V_HEADS].reshape(t, GDN_K_HEADS, 2)
    gates = jnp.concatenate([a, bt, jnp.zeros((t, GDN_K_HEADS, 4), F32)], axis=-1).transpose(1, 2, 0)
    conv0_pad = jnp.pad(state_conv[0], ((0, 0), (SUBLANES - (CONV_W - 1), 0), (0, 0)))
    o_p, gdn_p = _gdn(proj, gates, 0, bp, lp, None, None, gdn_conv[0], gdn_a_log[0], gdn_dt_bias[0],
                      gdn_norm[0], chunk=GDN_CHUNK, tb=GDN_TB, bb=1)
    o_s, gdn_s = _gdn(proj, gates, tp, bs, ls, conv0_pad, state_gdn[0], gdn_conv[0], gdn_a_log[0],
                      gdn_dt_bias[0], gdn_norm[0], chunk=ls, tb=ls, bb=GDN_SAMPLE_BB)
    proj8 = proj.reshape(t // SUBLANES, SUBLANES, n_main)
    keep = SUBLANES - (CONV_W - 1)
    conv_p = jnp.concatenate(
        [lax.slice(proj8, ((b * lp + lp) // SUBLANES - 1, keep, 0),
                   ((b * lp + lp) // SUBLANES, SUBLANES, CONV_DIM)) for b in range(bp)], axis=0)
    conv_s = lax.slice(proj8, ((tp + ls) // SUBLANES - 1, keep, 0), (t // SUBLANES, SUBLANES, CONV_DIM),
                       (ls // SUBLANES, 1, 1))
    x = _matmul((o_p, o_s), gdn_w_out.reshape(hv, d), d, ROW_TILE, MM_TN_OUT, res=x)
    x = _hier_moe(x, norm_ffn[1], moe_w_group[1], moe_b_group[1], moe_w_expert[1], moe_b_expert[1],
                  moe_w_in, moe_w_out, 1)

    y_p = _rmsnorm_rows(x, norm_final, F32, 0, tp)
    y_s = _rmsnorm_rows(x, norm_final, F32, tp, ts)
    return (y_p.reshape(bp, lp, d), y_s.reshape(bs, ls, d),
            ret_p[None], ret_s[None], gdn_p[None], gdn_s[None], conv_p[None], conv_s[None])
```

```python
import functools
import math

import numpy as np
import jax
import jax.numpy as jnp
from jax import lax
from jax.experimental import pallas as pl
from jax.experimental.pallas import tpu as pltpu

F32 = jnp.float32
BF16 = jnp.bfloat16

D_MODEL = 2048
PAST_LEN = 16384
RET_HEADS = 8
RET_DK = 256
RET_DV = 512
ROPE_BASE = 10000.0
GDN_K_HEADS = 16
GDN_V_HEADS = 32
GDN_DK = 128
GDN_DV = 128
CONV_W = 4
CONV_DIM = 2 * GDN_K_HEADS * GDN_DK + GDN_V_HEADS * GDN_DV
MOE_GROUPS = 8
EXPERTS_PER_GROUP = 8
N_EXPERTS = MOE_GROUPS * EXPERTS_PER_GROUP
TOP_K = 2
D_EXPERT = D_MODEL // 4
EPS = 1e-6

LANES = 128
SUBLANES = 8
BF16_ROWS = 16
VMEM_LIMIT_BYTES = 56 * 1024 * 1024

ROW_TILE = 512
MM_TM = 1024
MM_TN = 1024
MM_TN_OUT = 512
RET_CHUNK = 128
RET_TB = 512
RET_SAMPLE_BB = 8
GDN_CHUNK = 64
GDN_TB = 1024
GDN_GROUP = 128
GDN_SAMPLE_BB = 32
MOE_BLOCK = 128
COMBINE_TM = 128
GATHER_UNROLL = 8
MOE_W_SLABS = 4
BULK_DMA_PRIORITY = 1


def _params(sem):
    return pltpu.CompilerParams(dimension_semantics=sem, vmem_limit_bytes=VMEM_LIMIT_BYTES)


def _dot(a, b):
    return jnp.dot(a, b, preferred_element_type=F32)


def _dot_nt(a, b):
    return lax.dot_general(a, b, (((1,), (1,)), ((), ())), preferred_element_type=F32)


def _dot_tn(a, b):
    return lax.dot_general(a, b, (((0,), (0,)), ((), ())), preferred_element_type=F32)


def _split3(x):
    hi = x.astype(BF16)
    r1 = x - hi.astype(F32)
    mid = r1.astype(BF16)
    lo = (r1 - mid.astype(F32)).astype(BF16)
    return hi, mid, lo


def _silu(x):
    return x * (1.0 / (1.0 + jnp.exp(-x)))


def _row_segments(operands, tile):
    stacks = [op if isinstance(op, (tuple, list)) else (op,) for op in operands]
    cuts = set()
    for parts in stacks:
        edge = 0
        for p in parts:
            assert p.shape[0] % tile == 0
            cuts.add(edge)
            edge += p.shape[0] // tile
        cuts.add(edge)
    cuts = sorted(cuts)
    segments = [(a, b - a) for a, b in zip(cuts[:-1], cuts[1:])]
    sources = []
    for parts in stacks:
        assert sum(p.shape[0] for p in parts) == cuts[-1] * tile
        src, edge = [], 0
        for p in parts:
            for first, _ in segments:
                if edge <= first < edge + p.shape[0] // tile:
                    src.append((p, first - edge))
            edge += p.shape[0] // tile
        sources.append(src)
    return segments, sources


def _segment_spec(block, segment, source_tile0, row_axis, col_of):
    first, n_tiles = segment

    def index_map(*ids):
        return (source_tile0 + jnp.clip(ids[row_axis] - first, 0, n_tiles - 1), col_of(*ids))
    return pl.BlockSpec(block, index_map)


def _for_current_segment(segments, tile_id, fn):
    if len(segments) == 1:
        fn(0)
        return
    for k, (first, n_tiles) in enumerate(segments):
        pl.when(jnp.logical_and(tile_id >= first, tile_id < first + n_tiles))(functools.partial(fn, k))


def _rmsnorm_kernel(*refs, segments):
    x_refs, (w_ref, o_ref) = refs[:len(segments)], refs[len(segments):]

    def run(k):
        x = x_refs[k][...]
        ms = jnp.mean(x * x, axis=-1, keepdims=True)
        o_ref[...] = (x * lax.rsqrt(ms + EPS) * w_ref[...]).astype(o_ref.dtype)

    _for_current_segment(segments, pl.program_id(0), run)


def _rmsnorm(x, w, out_dtype):
    segments, (src,) = _row_segments([x], ROW_TILE)
    d = src[0][0].shape[1]
    n_tiles = segments[-1][0] + segments[-1][1]
    return pl.pallas_call(
        functools.partial(_rmsnorm_kernel, segments=segments),
        grid=(n_tiles,),
        in_specs=[_segment_spec((ROW_TILE, d), seg, t0, 0, lambda i: 0)
                  for seg, (_, t0) in zip(segments, src)]
        + [pl.BlockSpec((1, d), lambda i: (0, 0))],
        out_specs=pl.BlockSpec((ROW_TILE, d), lambda i: (i, 0)),
        out_shape=jax.ShapeDtypeStruct((n_tiles * ROW_TILE, d), out_dtype),
        compiler_params=_params(("arbitrary",)),
        name="rmsnorm",
    )(*[p for p, _ in src], w.reshape(1, d))


def _rmsnorm_rows(x, w, out_dtype, row0, n_rows):
    d = x.shape[1]
    assert row0 % ROW_TILE == 0 and n_rows % ROW_TILE == 0
    seg = (0, n_rows // ROW_TILE)
    return pl.pallas_call(
        functools.partial(_rmsnorm_kernel, segments=[seg]),
        grid=(n_rows // ROW_TILE,),
        in_specs=[_segment_spec((ROW_TILE, d), seg, row0 // ROW_TILE, 0, lambda i: 0),
                  pl.BlockSpec((1, d), lambda i: (0, 0))],
        out_specs=pl.BlockSpec((ROW_TILE, d), lambda i: (i, 0)),
        out_shape=jax.ShapeDtypeStruct((n_rows, d), out_dtype),
        compiler_params=_params(("arbitrary",)),
        name="rmsnorm",
    )(x, w.reshape(1, d))


def _matmul_kernel(*refs, segments, has_res):
    n = len(segments)
    a_refs, w_ref = refs[:n], refs[n]
    r_refs = refs[n + 1:2 * n + 1] if has_res else None
    o_ref, wb_ref = refs[-2:]
    i = pl.program_id(1)

    @pl.when(i == 0)
    def _():
        wb_ref[...] = w_ref[...].astype(BF16)

    def run(k):
        acc = _dot(a_refs[k][...], wb_ref[...])
        o_ref[...] = acc + r_refs[k][...] if has_res else acc

    _for_current_segment(segments, i, run)


def _matmul(a, w, n_cols, tm, tn, res=None, layer=None):
    segments, sources = _row_segments([a] if res is None else [a, res], tm)
    k = sources[0][0][0].shape[1]
    m_tiles = segments[-1][0] + segments[-1][1]
    in_specs = [_segment_spec((tm, k), seg, t0, 1, lambda j, i: 0)
                for seg, (_, t0) in zip(segments, sources[0])]
    if layer is None:
        in_specs.append(pl.BlockSpec((k, tn), lambda j, i: (0, j)))
    else:
        in_specs.append(pl.BlockSpec((None, k, tn), lambda j, i: (layer, 0, j)))
    args = [p for p, _ in sources[0]] + [w]
    if res is not None:
        in_specs += [_segment_spec((tm, tn), seg, t0, 1, lambda j, i: j)
                     for seg, (_, t0) in zip(segments, sources[1])]
        args += [p for p, _ in sources[1]]
    return pl.pallas_call(
        functools.partial(_matmul_kernel, segments=segments, has_res=res is not None),
        grid=(n_cols // tn, m_tiles),
        in_specs=in_specs,
        out_specs=pl.BlockSpec((tm, tn), lambda j, i: (i, j)),
        out_shape=jax.ShapeDtypeStruct((m_tiles * tm, n_cols), F32),
        scratch_shapes=[pltpu.VMEM((k, tn), BF16)],
        compiler_params=_params(("arbitrary", "arbitrary")),
        name="matmul",
    )(*args)


def _ret_kernel(lg_ref, q_ref, k_ref, v_ref, g_ref, cos_ref, sin_ref, gn_ref, *rest,
                chunk, n_seq, chunks_per_seq, has_s0):
    if has_s0:
        s0a_ref, s0b_ref, o_ref, s_ref = rest
    else:
        o_ref, s_ref = rest
    c = chunk
    h = pl.program_id(1)
    lg = lg_ref[h]

    if not has_s0:
        @pl.when(pl.program_id(2) == 0)
        def _():
            s_ref[...] = jnp.zeros(s_ref.shape, F32)

    ii = lax.broadcasted_iota(jnp.int32, (c, 1), 0).astype(F32)
    jj = lax.broadcasted_iota(jnp.int32, (1, c), 1).astype(F32)
    diff = ii - jj
    causal = diff >= 0.0
    decay = jnp.where(causal, jnp.exp(lg * jnp.where(causal, diff, 0.0)), 0.0)
    q_decay = jnp.exp(lg * (ii + 1.0))
    k_decay = jnp.exp(lg * (c - 1.0 - ii))
    chunk_decay = jnp.exp(jnp.full((1, 1), lg * c, F32))
    half = RET_DK // 2
    k_scale = RET_DK ** -0.5
    gn_w = gn_ref[...]
    op = (lambda t: t.astype(BF16)) if c % BF16_ROWS == 0 else (lambda t: t)
    outs = []

    for s in range(n_seq):
        state = jnp.concatenate([s0a_ref[s], s0b_ref[s]], axis=0) if has_s0 else s_ref[s]
        for ci in range(chunks_per_seq):
            r0 = (s * chunks_per_seq + ci) * c
            rows = slice(r0, r0 + c)
            cos = cos_ref[rows, :]
            sin = sin_ref[rows, :]
            q = q_ref[rows, :]
            k = k_ref[rows, :]
            q1, q2 = q[:, :half], q[:, half:]
            k1, k2 = k[:, :half], k[:, half:]
            qr = jnp.concatenate([q1 * cos - q2 * sin, q1 * sin + q2 * cos], axis=-1)
            kr = jnp.concatenate([k1 * cos - k2 * sin, k1 * sin + k2 * cos], axis=-1) * k_scale
            vb = op(v_ref[rows, :])
            qb = op(qr)
            scores = _dot_nt(qb, op(kr)) * decay
            inner = _dot(op(scores), vb)
            cross = _dot(qb, op(state)) * q_decay
            o = inner + cross
            state = state * chunk_decay + _dot_tn(op(kr * k_decay), vb)
            mu = jnp.mean(o, axis=-1, keepdims=True)
            oc = o - mu
            var = jnp.mean(oc * oc, axis=-1, keepdims=True)
            on = oc * lax.rsqrt(var + EPS) * gn_w
            res = on * _silu(g_ref[rows, :])
            if c % BF16_ROWS == 0:
                o_ref[rows, :] = res.astype(o_ref.dtype)
            else:
                outs.append(res)
        s_ref[s] = state
    if outs:
        o_ref[...] = jnp.concatenate(outs, axis=0).astype(o_ref.dtype)


def _retention(proj, row0, n_seq_total, seq_len, pos0, s0, gn_w, *, chunk, tb, bb):
    hq = RET_HEADS * RET_DK
    n_rows = n_seq_total * seq_len
    blk_rows = bb * tb
    n_inner = seq_len // tb
    n_outer = n_seq_total // bb
    assert row0 % blk_rows == 0 and seq_len % tb == 0 and tb % chunk == 0
    assert s0 is None or n_inner == 1
    rb0 = row0 // blk_rows
    half = RET_DK // 2

    pos = (pos0 + jnp.arange(seq_len)).astype(F32)
    inv = ROPE_BASE ** (-jnp.arange(half, dtype=F32) / half)
    ang = pos[:, None] * inv[None, :]
    cos = jnp.tile(jnp.cos(ang), (bb, 1)) if n_inner == 1 else jnp.cos(ang)
    sin = jnp.tile(jnp.sin(ang), (bb, 1)) if n_inner == 1 else jnp.sin(ang)
    log_gamma = jnp.log1p(-jnp.exp2(-5.0 - jnp.arange(RET_HEADS, dtype=F32)))

    def row_map(b, h, n):
        return rb0 + b * n_inner + n

    in_specs = [
        pl.BlockSpec(memory_space=pltpu.SMEM),
        pl.BlockSpec((blk_rows, RET_DK), lambda b, h, n: (row_map(b, h, n), h)),
        pl.BlockSpec((blk_rows, RET_DK), lambda b, h, n: (row_map(b, h, n), RET_HEADS + h)),
        pl.BlockSpec((blk_rows, RET_DV), lambda b, h, n: (row_map(b, h, n), 2 * hq // RET_DV + h)),
        pl.BlockSpec((blk_rows, RET_DV), lambda b, h, n: (row_map(b, h, n), 2 * hq // RET_DV + RET_HEADS + h)),
        pl.BlockSpec((blk_rows, half), lambda b, h, n: (n, 0)),
        pl.BlockSpec((blk_rows, half), lambda b, h, n: (n, 0)),
        pl.BlockSpec((None, 1, RET_DV), lambda b, h, n: (h, 0, 0)),
    ]
    args = [log_gamma, proj, proj, proj, proj, cos, sin, gn_w.reshape(RET_HEADS, 1, RET_DV)]
    if s0 is not None:
        in_specs += [pl.BlockSpec((bb, None, RET_DK // 2, RET_DV), lambda b, h, n: (b, h, 0, 0)),
                     pl.BlockSpec((bb, None, RET_DK // 2, RET_DV), lambda b, h, n: (b, h, 1, 0))]
        args += [s0, s0]
    out, s_fin = pl.pallas_call(
        functools.partial(_ret_kernel, chunk=chunk, n_seq=bb, chunks_per_seq=tb // chunk,
                          has_s0=s0 is not None),
        grid=(n_outer, RET_HEADS, n_inner),
        in_specs=in_specs,
        out_specs=[pl.BlockSpec((blk_rows, RET_DV), lambda b, h, n: (b * n_inner + n, h)),
                   pl.BlockSpec((bb, None, RET_DK, RET_DV), lambda b, h, n: (b, h, 0, 0))],
        out_shape=[jax.ShapeDtypeStruct((n_rows, RET_HEADS * RET_DV), BF16),
                   jax.ShapeDtypeStruct((n_seq_total, RET_HEADS, RET_DK, RET_DV), F32)],
        compiler_params=_params(("parallel", "parallel", "arbitrary")),
        name="retention",
    )(*args)
    return out, s_fin


def _shift_rows(x, prev8, s, x_ref=None):
    n = x.shape[0]
    first = x[:SUBLANES]
    row = lax.broadcasted_iota(jnp.int32, first.shape, 0)
    top = jnp.where(row < s, pltpu.roll(prev8, s, 0), pltpu.roll(first, s, 0))
    if n == SUBLANES:
        return top
    if x_ref is None:
        rest = pltpu.roll(x, s, 0)[SUBLANES:]
    else:
        rest = x_ref[pl.ds(SUBLANES - s, n - SUBLANES), :]
    return jnp.concatenate([top, rest], axis=0)


def _causal_conv_silu(x, prev8, cw, x_ref=None):
    acc = x * cw[CONV_W - 1:CONV_W, :]
    for s in range(1, CONV_W):
        acc = acc + _shift_rows(x, prev8, s, x_ref) * cw[CONV_W - 1 - s:CONV_W - s, :]
    return _silu(acc)


def _gdn_kernel(q_ref, k_ref, v_ref, z_ref, gb_ref, alog_ref, dtb_ref, cq_ref, ck_ref, cv_ref,
                nw_ref, *rest, chunk, group, n_seq, rows_per_seq, carry):
    has_init = not carry
    if has_init:
        pq_ref, pk_ref, pv_ref, s0_ref, o_ref, s_ref, prev_ref = rest
    else:
        o_ref, s_ref, prev_ref = rest
    c = chunk
    r = n_seq * rows_per_seq
    m = group
    dk = GDN_DK
    op = (lambda t: t.astype(BF16)) if c % BF16_ROWS == 0 else (lambda t: t)

    if carry:
        @pl.when(pl.program_id(2) == 0)
        def _():
            s_ref[...] = jnp.zeros(s_ref.shape, F32)
            prev_ref[...] = jnp.zeros(prev_ref.shape, F32)

    q_raw, k_raw, v_raw = q_ref[...], k_ref[...], v_ref[...]
    if carry:
        prev = prev_ref[...]
        qc = _causal_conv_silu(q_raw, prev[:, :dk], cq_ref[...], q_ref)
        kc = _causal_conv_silu(k_raw, prev[:, dk:2 * dk], ck_ref[...], k_ref)
        vc = _causal_conv_silu(v_raw, prev[:, 2 * dk:], cv_ref[...], v_ref)
        prev_ref[...] = jnp.concatenate(
            [q_raw[r - SUBLANES:], k_raw[r - SUBLANES:], v_raw[r - SUBLANES:]], axis=-1)
    else:
        qs, ks, vs = [], [], []
        for s in range(n_seq):
            rows = slice(s * rows_per_seq, (s + 1) * rows_per_seq)
            qs.append(_causal_conv_silu(q_raw[rows], pq_ref[s], cq_ref[...]))
            ks.append(_causal_conv_silu(k_raw[rows], pk_ref[s], ck_ref[...]))
            vs.append(_causal_conv_silu(v_raw[rows], pv_ref[s], cv_ref[...]))
        qc = jnp.concatenate(qs, axis=0)
        kc = jnp.concatenate(ks, axis=0)
        vc = jnp.concatenate(vs, axis=0)

    qn = qc * lax.rsqrt(jnp.sum(qc * qc, axis=-1, keepdims=True) + EPS) * (dk ** -0.5)
    kn = kc * lax.rsqrt(jnp.sum(kc * kc, axis=-1, keepdims=True) + EPS)
    qb = qn.astype(BF16)
    kb = kn.astype(BF16)

    gb = gb_ref[...]
    a_coef = jnp.exp(alog_ref[...][:, :1])
    xs = gb + dtb_ref[...][:, :1]
    softplus = jnp.maximum(xs, 0.0) + jnp.log(1.0 + jnp.exp(-jnp.abs(xs)))
    g_rows = -a_coef * softplus
    beta_rows = 1.0 / (1.0 + jnp.exp(-gb))
    rid = lax.broadcasted_iota(jnp.int32, gb.shape, 0)
    x8 = jnp.where(rid < 2, g_rows, beta_rows)

    ri = lax.broadcasted_iota(jnp.int32, (m, m), 0)
    ci = lax.broadcasted_iota(jnp.int32, (m, m), 1)
    log2c = int(math.log2(c))
    same = lax.shift_right_logical(ri, log2c) == lax.shift_right_logical(ci, log2c)
    tri_incl = jnp.logical_and(same, ri >= ci)
    tri_strict = jnp.logical_and(same, ri > ci)
    tri_b = jnp.where(tri_incl, 1.0, 0.0).astype(BF16)
    eye_b = jnp.where(ri == ci, 1.0, 0.0).astype(BF16)
    eye_f = jnp.where(ri == ci, 1.0, 0.0)
    n_sq = max(log2c - 1, 0)
    nw = nw_ref[...]

    units = [(g, vh) for g in range(r // m) for vh in range(2)]
    rows_of = {g: slice(g * m, (g + 1) * m) for g in range(r // m)}
    gate = {}
    for g in range(r // m):
        rows_g = rows_of[g]
        parts = _split3(x8[:, rows_g])
        cs_col = sum(_dot_nt(tri_b, p) for p in parts)
        cs_row = sum(_dot_nt(p, tri_b) for p in parts)
        x_col = sum(_dot_nt(eye_b, p) for p in parts)
        gate[g] = (cs_col, cs_row, x_col, jnp.exp(cs_col),
                   _dot_nt(kb[rows_g], kb[rows_g]), _dot_nt(qb[rows_g], kb[rows_g]))

    g_col, b_col, eg_col, dmat, t_mat, pw = {}, {}, {}, {}, {}, {}
    for u in units:
        g, vh = u
        cs_col, cs_row, x_col, exp_col, kk, _ = gate[g]
        g_col[u] = cs_col[:, vh:vh + 1]
        b_col[u] = x_col[:, 2 + vh:3 + vh]
        eg_col[u] = exp_col[:, vh:vh + 1]
        dmat[u] = jnp.exp(jnp.where(tri_incl, g_col[u] - cs_row[vh:vh + 1, :], -jnp.inf))
        pw[u] = jnp.where(tri_strict, -(kk * dmat[u]) * b_col[u], 0.0)
        t_mat[u] = eye_f + pw[u]
    for _ in range(n_sq):
        for u in units:
            pwb = pw[u].astype(BF16)
            pw[u] = _dot(pwb, pwb)
        for u in units:
            t_mat[u] = t_mat[u] + _dot(t_mat[u].astype(BF16), pw[u].astype(BF16))

    sol, q_in = {}, {}
    for u in units:
        g, vh = u
        v_h = vc[rows_of[g], vh * GDN_DV:(vh + 1) * GDN_DV]
        rhs = jnp.concatenate([v_h * b_col[u], kn[rows_of[g]] * (b_col[u] * eg_col[u])], axis=-1)
        sol[u] = _dot(t_mat[u].astype(BF16), rhs.astype(BF16))
        q_in[u] = qn[rows_of[g]] * eg_col[u]

    chunks = [(u, ch) for u in units for ch in range(m // c)]
    k_dec, g_last, qp = {}, {}, {}
    for uc in chunks:
        u, ch = uc
        rows = slice(ch * c, (ch + 1) * c)
        g_last[uc] = g_col[u][(ch + 1) * c - 1:(ch + 1) * c, :]
        k_dec[uc] = kn[rows_of[u[0]]][rows] * jnp.exp(g_last[uc] - g_col[u][rows])
        if carry:
            qp[uc] = _dot_tn(k_dec[uc].astype(BF16), sol[u][rows].astype(BF16))

    v_new, o_inter = {}, {}
    if carry:
        states = [s_ref[0, vh] for vh in range(2)]
        sb = {}
        for g in range(r // m):
            for ch in range(m // c):
                for vh in range(2):
                    uc = ((g, vh), ch)
                    sb[uc] = states[vh].astype(BF16)
                    states[vh] = (states[vh] * jnp.exp(g_last[uc])
                                  - _dot(qp[uc][:, GDN_DV:].astype(BF16), sb[uc]) + qp[uc][:, :GDN_DV])
        for vh in range(2):
            s_ref[0, vh] = states[vh]
        for uc in chunks:
            u, ch = uc
            rows = slice(ch * c, (ch + 1) * c)
            both = _dot(jnp.concatenate([sol[u][rows, GDN_DV:].astype(BF16),
                                         q_in[u][rows].astype(BF16)], axis=0), sb[uc])
            v_new[uc] = sol[u][rows, :GDN_DV] - both[:c]
            o_inter[uc] = both[c:]
    else:
        for uc in chunks:
            u, ch = uc
            rows = slice(ch * c, (ch + 1) * c)
            seq = (u[0] * m + ch * c) // rows_per_seq
            state = s0_ref[seq, u[1]]
            both = _dot(jnp.concatenate([sol[u][rows, GDN_DV:], q_in[u][rows]], axis=0).astype(BF16),
                        state.astype(BF16))
            v_new[uc] = sol[u][rows, :GDN_DV] - both[:c]
            o_inter[uc] = both[c:]
            s_ref[seq, u[1]] = (state * jnp.exp(g_last[uc])
                                + _dot_tn(op(k_dec[uc]), op(v_new[uc])))

    for u in units:
        g, vh = u
        v_new_all = jnp.concatenate([v_new[(u, ch)] for ch in range(m // c)], axis=0)
        o = jnp.concatenate([o_inter[(u, ch)] for ch in range(m // c)], axis=0)
        attn = jnp.where(tri_incl, gate[g][5] * dmat[u], 0.0)
        o = o + _dot(attn.astype(BF16), v_new_all.astype(BF16))
        o = o * lax.rsqrt(jnp.mean(o * o, axis=-1, keepdims=True) + EPS) * nw
        z = z_ref[rows_of[g], vh * GDN_DV:(vh + 1) * GDN_DV]
        o_ref[rows_of[g], vh * GDN_DV:(vh + 1) * GDN_DV] = (o * _silu(z)).astype(o_ref.dtype)


def _gdn(proj, gates, row0, n_seq_total, seq_len, conv0_pad, s0, conv_w, a_log, dt_bias, norm_w,
         *, chunk, tb, bb):
    hk = GDN_K_HEADS * GDN_DK
    hv = GDN_V_HEADS * GDN_DV
    n_rows = n_seq_total * seq_len
    blk_rows = bb * tb
    n_inner = seq_len // tb
    n_outer = n_seq_total // bb
    carry = s0 is None
    assert row0 % blk_rows == 0 and seq_len % tb == 0 and tb % chunk == 0
    assert carry == (bb == 1) and (carry or (n_inner == 1 and tb == chunk))
    assert blk_rows % GDN_GROUP == 0 and GDN_GROUP % chunk == 0
    rb0 = row0 // blk_rows
    two = 2 * GDN_DV

    def row_map(b, h, n):
        return rb0 + b * n_inner + n

    alog8 = jnp.zeros((GDN_K_HEADS, SUBLANES, LANES), F32).at[:, :2, :].set(
        jnp.broadcast_to(a_log.reshape(GDN_K_HEADS, 2, 1), (GDN_K_HEADS, 2, LANES)))
    dtb8 = jnp.zeros((GDN_K_HEADS, SUBLANES, LANES), F32).at[:, :2, :].set(
        jnp.broadcast_to(dt_bias.reshape(GDN_K_HEADS, 2, 1), (GDN_K_HEADS, 2, LANES)))

    in_specs = [
        pl.BlockSpec((blk_rows, GDN_DK), lambda b, h, n: (row_map(b, h, n), h)),
        pl.BlockSpec((blk_rows, GDN_DK), lambda b, h, n: (row_map(b, h, n), GDN_K_HEADS + h)),
        pl.BlockSpec((blk_rows, two), lambda b, h, n: (row_map(b, h, n), 2 * hk // two + h)),
        pl.BlockSpec((blk_rows, two), lambda b, h, n: (row_map(b, h, n), CONV_DIM // two + h)),
        pl.BlockSpec((None, SUBLANES, blk_rows), lambda b, h, n: (h, 0, row_map(b, h, n))),
        pl.BlockSpec((None, SUBLANES, LANES), lambda b, h, n: (h, 0, 0)),
        pl.BlockSpec((None, SUBLANES, LANES), lambda b, h, n: (h, 0, 0)),
        pl.BlockSpec((CONV_W, GDN_DK), lambda b, h, n: (0, h)),
        pl.BlockSpec((CONV_W, GDN_DK), lambda b, h, n: (0, GDN_K_HEADS + h)),
        pl.BlockSpec((CONV_W, two), lambda b, h, n: (0, 2 * hk // two + h)),
        pl.BlockSpec((1, GDN_DV), lambda b, h, n: (0, 0)),
    ]
    args = [proj, proj, proj, proj, gates, alog8, dtb8, conv_w, conv_w, conv_w,
            norm_w.reshape(1, GDN_DV)]
    if not carry:
        in_specs += [
            pl.BlockSpec((bb, SUBLANES, GDN_DK), lambda b, h, n: (b, 0, h)),
            pl.BlockSpec((bb, SUBLANES, GDN_DK), lambda b, h, n: (b, 0, GDN_K_HEADS + h)),
            pl.BlockSpec((bb, SUBLANES, two), lambda b, h, n: (b, 0, 2 * hk // two + h)),
            pl.BlockSpec((bb, 2, GDN_DK, GDN_DV), lambda b, h, n: (b, h, 0, 0)),
        ]
        args += [conv0_pad, conv0_pad, conv0_pad, s0]
    out, s_fin = pl.pallas_call(
        functools.partial(_gdn_kernel, chunk=chunk, group=GDN_GROUP, n_seq=bb, rows_per_seq=tb,
                          carry=carry),
        grid=(n_outer, GDN_K_HEADS, n_inner),
        in_specs=in_specs,
        out_specs=[pl.BlockSpec((blk_rows, two), lambda b, h, n: (b * n_inner + n, h)),
                   pl.BlockSpec((bb, 2, GDN_DK, GDN_DV), lambda b, h, n: (b, h, 0, 0))],
        out_shape=[jax.ShapeDtypeStruct((n_rows, hv), BF16),
                   jax.ShapeDtypeStruct((n_seq_total, GDN_V_HEADS, GDN_DK, GDN_DV), F32)],
        scratch_shapes=[pltpu.VMEM((SUBLANES, 2 * GDN_DK + two), F32)],
        compiler_params=_params(("parallel", "parallel", "arbitrary")),
        name="gated_delta",
    )(*args)
    return out, s_fin


def _router_kernel(x_ref, w_ref, wr_ref, br_ref, h_ref, lg_ref):
    x = x_ref[...]
    ms = jnp.mean(x * x, axis=-1, keepdims=True)
    h = x * lax.rsqrt(ms + EPS) * w_ref[...]
    h_ref[...] = h
    lg_ref[...] = _dot_nt(wr_ref[...].astype(BF16), h.astype(BF16)) + br_ref[...]


def _router(x, norm_w, w_route, b_route):
    t, d = x.shape
    return pl.pallas_call(
        _router_kernel,
        grid=(t // ROW_TILE,),
        in_specs=[pl.BlockSpec((ROW_TILE, d), lambda i: (i, 0)),
                  pl.BlockSpec((1, d), lambda i: (0, 0)),
                  pl.BlockSpec((LANES, d), lambda i: (0, 0)),
                  pl.BlockSpec((LANES, 1), lambda i: (0, 0))],
        out_specs=[pl.BlockSpec((ROW_TILE, d), lambda i: (i, 0)),
                   pl.BlockSpec((LANES, ROW_TILE), lambda i: (0, i))],
        out_shape=[jax.ShapeDtypeStruct((t, d), F32),
                   jax.ShapeDtypeStruct((LANES, t), F32)],
        compiler_params=_params(("parallel",)),
        name="moe_router",
    )(x, norm_w.reshape(1, d), w_route, b_route)


def _row_gather_start(idx_ref, base, n, src_hbm, dst_ref, sem):
    def body(r, carry):
        tok = idx_ref[base + r]
        pltpu.make_async_copy(src_hbm.at[pl.ds(tok, 1)], dst_ref.at[pl.ds(r, 1)], sem).start()
        return carry
    lax.fori_loop(0, n, body, 0, unroll=GATHER_UNROLL)


def _row_gather_wait(n, src_hbm, dst_ref, sem):
    pltpu.make_async_copy(src_hbm.at[pl.ds(0, n)], dst_ref, sem).wait()


def _moe_ffn_kernel(be_ref, first_ref, nxt_ref, ws_ref, rt_ref, na_ref, h_hbm, rw_ref, win_hbm, wout_hbm,
                    o_ref, xbuf, sem, win_f, wout_f, win_sem, wout_sem, winb, woutb, *, layer):
    i = pl.program_id(0)
    n_active = na_ref[0]
    slot = lax.rem(i, 2)

    def weight_copies(e, ws):
        r_in = win_f.shape[1] // MOE_W_SLABS
        r_out = wout_f.shape[1] // MOE_W_SLABS
        cps = []
        for k in range(MOE_W_SLABS):
            cps.append(pltpu.make_async_copy(win_hbm.at[layer, e, pl.ds(k * r_in, r_in)],
                                             win_f.at[ws, pl.ds(k * r_in, r_in)], win_sem.at[ws, k]))
            cps.append(pltpu.make_async_copy(wout_hbm.at[layer, e, pl.ds(k * r_out, r_out)],
                                             wout_f.at[ws, pl.ds(k * r_out, r_out)], wout_sem.at[ws, k]))
        return cps

    @pl.when(jnp.logical_and(i == 0, n_active > 0))
    def _():
        for cp in weight_copies(be_ref[0], ws_ref[0]):
            cp.start(priority=BULK_DMA_PRIORITY)
        _row_gather_start(rt_ref, 0, MOE_BLOCK, h_hbm, xbuf.at[0], sem.at[0])

    @pl.when(i + 1 < n_active)
    def _():
        _row_gather_start(rt_ref, (i + 1) * MOE_BLOCK, MOE_BLOCK, h_hbm,
                          xbuf.at[1 - slot], sem.at[1 - slot])

    @pl.when(i < n_active)
    def _():
        ws = ws_ref[i]

        @pl.when(first_ref[i] == 1)
        def _():
            @pl.when(nxt_ref[i] >= 0)
            def _():
                for cp in weight_copies(nxt_ref[i], 1 - ws):
                    cp.start(priority=BULK_DMA_PRIORITY)

            for cp in weight_copies(be_ref[i], ws):
                cp.wait()
            winb[...] = win_f[ws].astype(BF16)
            woutb[...] = wout_f[ws].astype(BF16)

        _row_gather_wait(MOE_BLOCK, h_hbm, xbuf.at[slot], sem.at[slot])
        x = xbuf[slot].astype(BF16)
        mid = _dot(x, winb[...])
        act = (_silu(mid[:, :D_EXPERT]) * mid[:, D_EXPERT:]).astype(BF16)
        o_ref[...] = _dot(act, woutb[...]) * rw_ref[...]

    @pl.when(i >= n_active)
    def _():
        o_ref[...] = jnp.zeros(o_ref.shape, F32)


def _moe_ffn(h, block_e, block_first, block_next, block_wslot, row_tok, n_active, row_w,
             w_in, w_out, layer):
    t, d = h.shape
    n_blocks = block_e.shape[0]
    n_rows = n_blocks * MOE_BLOCK
    grid_spec = pltpu.PrefetchScalarGridSpec(
        num_scalar_prefetch=6,
        grid=(n_blocks,),
        in_specs=[
            pl.BlockSpec(memory_space=pl.ANY),
            pl.BlockSpec((MOE_BLOCK, 1), lambda i, *_: (i, 0)),
            pl.BlockSpec(memory_space=pl.ANY),
            pl.BlockSpec(memory_space=pl.ANY),
        ],
        out_specs=pl.BlockSpec((MOE_BLOCK, d), lambda i, *_: (i, 0)),
        scratch_shapes=[pltpu.VMEM((2, MOE_BLOCK, d), F32),
                        pltpu.SemaphoreType.DMA((2,)),
                        pltpu.VMEM((2, d, 2 * D_EXPERT), F32),
                        pltpu.VMEM((2, D_EXPERT, d), F32),
                        pltpu.SemaphoreType.DMA((2, MOE_W_SLABS)),
                        pltpu.SemaphoreType.DMA((2, MOE_W_SLABS)),
                        pltpu.VMEM((d, 2 * D_EXPERT), BF16),
                        pltpu.VMEM((D_EXPERT, d), BF16)],
    )
    return pl.pallas_call(
        functools.partial(_moe_ffn_kernel, layer=layer),
        grid_spec=grid_spec,
        out_shape=jax.ShapeDtypeStruct((n_rows, d), F32),
        compiler_params=_params(("arbitrary",)),
        name="moe_ffn",
    )(block_e, block_first, block_next, block_wslot, row_tok, n_active, h,
      row_w.reshape(n_rows, 1), w_in, w_out)


def _combine_kernel(pos_ref, ys_hbm, x_ref, o_ref, buf, sem):
    i = pl.program_id(0)
    n = pl.num_programs(0)
    slot = lax.rem(i, 2)
    rows = 2 * COMBINE_TM

    @pl.when(i == 0)
    def _():
        _row_gather_start(pos_ref, 0, rows, ys_hbm, buf.at[0], sem.at[0])

    @pl.when(i + 1 < n)
    def _():
        _row_gather_start(pos_ref, (i + 1) * rows, rows, ys_hbm, buf.at[1 - slot], sem.at[1 - slot])

    _row_gather_wait(rows, ys_hbm, buf.at[slot], sem.at[slot])
    o_ref[...] = x_ref[...] + buf[slot, :COMBINE_TM, :] + buf[slot, COMBINE_TM:, :]


def _combine(x, ys, pos):
    t, d = x.shape
    grid_spec = pltpu.PrefetchScalarGridSpec(
        num_scalar_prefetch=1,
        grid=(t // COMBINE_TM,),
        in_specs=[pl.BlockSpec(memory_space=pl.ANY),
                  pl.BlockSpec((COMBINE_TM, d), lambda i, p: (i, 0))],
        out_specs=pl.BlockSpec((COMBINE_TM, d), lambda i, p: (i, 0)),
        scratch_shapes=[pltpu.VMEM((2, 2 * COMBINE_TM, d), F32),
                        pltpu.SemaphoreType.DMA((2,))],
    )
    return pl.pallas_call(
        _combine_kernel,
        grid_spec=grid_spec,
        out_shape=jax.ShapeDtypeStruct((t, d), F32),
        compiler_params=_params(("arbitrary",)),
        name="moe_combine",
    )(pos, ys, x)


def _hier_moe(x, norm_w, w_group, b_group, w_expert, b_expert, w_in, w_out, layer):
    t, d = x.shape
    n_route = MOE_GROUPS + N_EXPERTS
    w_route = jnp.zeros((LANES, d), F32).at[:MOE_GROUPS].set(w_group.T).at[MOE_GROUPS:n_route].set(w_expert.T)
    b_route = jnp.zeros((LANES, 1), F32).at[:MOE_GROUPS, 0].set(b_group).at[MOE_GROUPS:n_route, 0].set(b_expert)
    h, logits = _router(x, norm_w, w_route, b_route)

    g_logits = logits[:MOE_GROUPS]
    g_max = jnp.max(g_logits, axis=0)
    g_sel = jnp.argmax(g_logits, axis=0).astype(jnp.int32)
    g_w = 1.0 / jnp.sum(jnp.exp(g_logits - g_max[None, :]), axis=0)
    e_all = logits[MOE_GROUPS:n_route].reshape(MOE_GROUPS, EXPERTS_PER_GROUP, t)
    grp = jnp.arange(MOE_GROUPS, dtype=jnp.int32)[:, None, None]
    e_logits = jnp.sum(jnp.where(grp == g_sel[None, None, :], e_all, 0.0), axis=0)
    slot_id = jnp.arange(EXPERTS_PER_GROUP, dtype=jnp.int32)[:, None]
    i1 = jnp.argmax(e_logits, axis=0).astype(jnp.int32)
    v1 = jnp.max(e_logits, axis=0)
    rest = jnp.where(slot_id == i1[None, :], -jnp.inf, e_logits)
    i2 = jnp.argmax(rest, axis=0).astype(jnp.int32)
    v2 = jnp.max(rest, axis=0)
    e21 = jnp.exp(v2 - v1)
    weights = jnp.stack([g_w / (1.0 + e21), g_w * e21 / (1.0 + e21)], axis=1)
    expert_id = jnp.stack([g_sel * EXPERTS_PER_GROUP + i1, g_sel * EXPERTS_PER_GROUP + i2], axis=1)

    n_assign = t * TOP_K
    flat_e = expert_id.reshape(n_assign)
    iota = jnp.arange(n_assign, dtype=jnp.int32)
    experts = jnp.arange(N_EXPERTS, dtype=jnp.int32)
    sorted_e, order, w_sorted = lax.sort((flat_e, iota, weights.reshape(n_assign)), num_keys=1,
                                         is_stable=True)
    counts = jnp.sum((flat_e[:, None] == experts[None, :]).astype(jnp.int32), axis=0)
    start = jnp.cumsum(counts) - counts
    nblk_e = (counts + MOE_BLOCK - 1) // MOE_BLOCK
    blk_end = jnp.cumsum(nblk_e)
    blk_start = blk_end - nblk_e

    def lookup(idx, table):
        return jnp.sum(jnp.where(idx[:, None] == experts[None, :], table[None, :], 0), axis=1)

    dest_sorted = iota + lookup(sorted_e, blk_start * MOE_BLOCK - start)
    _, dest = lax.sort((order, dest_sorted), num_keys=1)

    n_blocks = -(-n_assign // MOE_BLOCK) + N_EXPERTS
    n_active = blk_end[-1].astype(jnp.int32)
    blk = jnp.minimum(jnp.arange(n_blocks, dtype=jnp.int32), jnp.maximum(n_active - 1, 0))
    block_e = jnp.minimum(jnp.sum((blk[:, None] >= blk_end[None, :]).astype(jnp.int32), axis=1),
                          N_EXPERTS - 1)
    block_first = (blk == lookup(block_e, blk_start)).astype(jnp.int32)
    active = counts > 0
    ordinal = jnp.cumsum(active.astype(jnp.int32)) - 1
    block_wslot = lookup(block_e, ordinal) % 2
    later = lax.cummin(jnp.where(active, experts, N_EXPERTS), reverse=True)
    next_e = jnp.concatenate([later[1:], jnp.full((1,), N_EXPERTS, jnp.int32)])
    block_next = lookup(block_e, jnp.where(next_e >= N_EXPERTS, -1, next_e))

    j_base = lookup(block_e, start) + (blk - lookup(block_e, blk_start)) * MOE_BLOCK
    j_end = lookup(block_e, start + counts)
    j = j_base[:, None] + jnp.arange(MOE_BLOCK, dtype=jnp.int32)[None, :]
    valid = jnp.logical_and(j < j_end[:, None],
                            jnp.arange(n_blocks, dtype=jnp.int32)[:, None] < n_active)
    j = jnp.clip(j, 0, n_assign - 1).reshape(-1)
    valid = valid.reshape(-1)
    row_tok = jnp.where(valid, order[j] // TOP_K, 0)
    row_w = jnp.where(valid, w_sorted[j], 0.0)

    ys = _moe_ffn(h, block_e, block_first, block_next, block_wslot, row_tok, n_active.reshape(1),
                  row_w, w_in, w_out, layer)
    pos = dest.reshape(t // COMBINE_TM, COMBINE_TM, TOP_K).transpose(0, 2, 1).reshape(n_assign)
    return _combine(x, ys, pos)


def kernel(x_prompt, x_sample, state_ret, state_gdn, state_conv, norm_mix, norm_ffn, norm_final,
           ret_w_in, ret_gn, ret_w_out, gdn_w_in, gdn_conv, gdn_a_log, gdn_dt_bias, gdn_norm,
           gdn_w_out, moe_w_group, moe_b_group, moe_w_expert, moe_b_expert, moe_w_in, moe_w_out):
    bp, lp, d = x_prompt.shape
    bs, ls, _ = x_sample.shape
    tp, ts = bp * lp, bs * ls
    x = (x_prompt.reshape(tp, d), x_sample.reshape(ts, d))

    h = _rmsnorm(x, norm_mix[0], BF16)
    ret_in = ret_w_in.shape[-1]
    proj = _matmul(h, ret_w_in, ret_in, MM_TM, MM_TN, layer=0)
    o_p, ret_p = _retention(proj, 0, bp, lp, 0, None, ret_gn[0],
                            chunk=RET_CHUNK, tb=RET_TB, bb=1)
    o_s, ret_s = _retention(proj, tp, bs, ls, PAST_LEN, state_ret[0], ret_gn[0],
                            chunk=ls, tb=ls, bb=RET_SAMPLE_BB)
    x = _matmul((o_p, o_s), ret_w_out, d, ROW_TILE, MM_TN_OUT, res=x, layer=0)
    x = _hier_moe(x, norm_ffn[0], moe_w_group[0], moe_b_group[0], moe_w_expert[0], moe_b_expert[0],
                  moe_w_in, moe_w_out, 0)

    h = _rmsnorm(x, norm_mix[1], BF16)
    hv = GDN_V_HEADS * GDN_DV
    n_main = CONV_DIM + hv
    proj = _matmul(h, gdn_w_in, n_main, MM_TM, MM_TN, layer=0)
    w_tail = jnp.zeros((d, LANES), F32).at[:, :2 * GDN_V_HEADS].set(gdn_w_in[0, :, n_main:])
    tail = _matmul(h, w_tail, LANES, MM_TM, LANES)
    t = tp + ts
    bt = tail[:, :GDN_V_HEADS].reshape(t, GDN_K_HEADS, 2)
    a = tail[:, GDN_V_HEADS:2 * GDN_V_HEADS].reshape(t, GDN_K_HEADS, 2)
    gates = jnp.concatenate([a, bt, jnp.zeros((t, GDN_K_HEADS, 4), F32)], axis=-1).transpose(1, 2, 0)
    conv0_pad = jnp.pad(state_conv[0], ((0, 0), (SUBLANES - (CONV_W - 1), 0), (0, 0)))
    o_p, gdn_p = _gdn(proj, gates, 0, bp, lp, None, None, gdn_conv[0], gdn_a_log[0], gdn_dt_bias[0],
                      gdn_norm[0], chunk=GDN_CHUNK, tb=GDN_TB, bb=1)
    o_s, gdn_s = _gdn(proj, gates, tp, bs, ls, conv0_pad, state_gdn[0], gdn_conv[0], gdn_a_log[0],
                      gdn_dt_bias[0], gdn_norm[0], chunk=ls, tb=ls, bb=GDN_SAMPLE_BB)
    proj8 = proj.reshape(t // SUBLANES, SUBLANES, n_main)
    keep = SUBLANES - (CONV_W - 1)
    conv_p = jnp.concatenate(
        [lax.slice(proj8, ((b * lp + lp) // SUBLANES - 1, keep, 0),
                   ((b * lp + lp) // SUBLANES, SUBLANES, CONV_DIM)) for b in range(bp)], axis=0)
    conv_s = lax.slice(proj8, ((tp + ls) // SUBLANES - 1, keep, 0), (t // SUBLANES, SUBLANES, CONV_DIM),
                       (ls // SUBLANES, 1, 1))
    x = _matmul((o_p, o_s), gdn_w_out, d, ROW_TILE, MM_TN_OUT, res=x, layer=0)
    x = _hier_moe(x, norm_ffn[1], moe_w_group[1], moe_b_group[1], moe_w_expert[1], moe_b_expert[1],
                  moe_w_in, moe_w_out, 1)

    y_p = _rmsnorm_rows(x, norm_final, F32, 0, tp)
    y_s = _rmsnorm_rows(x, norm_final, F32, tp, ts)
    return (y_p.reshape(bp, lp, d), y_s.reshape(bs, ls, d),
            ret_p[None], ret_s[None], gdn_p[None], gdn_s[None], conv_p[None], conv_s[None])
```

```python
import functools
import math

import numpy as np
import jax
import jax.numpy as jnp
from jax import lax
from jax.experimental import pallas as pl
from jax.experimental.pallas import tpu as pltpu

F32 = jnp.float32
BF16 = jnp.bfloat16

D_MODEL = 2048
PAST_LEN = 16384
RET_HEADS = 8
RET_DK = 256
RET_DV = 512
ROPE_BASE = 10000.0
GDN_K_HEADS = 16
GDN_V_HEADS = 32
GDN_DK = 128
GDN_DV = 128
CONV_W = 4
CONV_DIM = 2 * GDN_K_HEADS * GDN_DK + GDN_V_HEADS * GDN_DV
MOE_GROUPS = 8
EXPERTS_PER_GROUP = 8
N_EXPERTS = MOE_GROUPS * EXPERTS_PER_GROUP
TOP_K = 2
D_EXPERT = D_MODEL // 4
EPS = 1e-6

LANES = 128
SUBLANES = 8
BF16_ROWS = 16
VMEM_LIMIT_BYTES = 56 * 1024 * 1024

ROW_TILE = 512
MM_TM = 1024
MM_TN = 1024
MM_TN_OUT = 512
RET_CHUNK = 128
RET_TB = 512
RET_SAMPLE_BB = 8
GDN_CHUNK = 64
GDN_TB = 1024
GDN_GROUP = 128
GDN_SAMPLE_BB = 32
MOE_BLOCK = 128
MOE_GATHER_SLOTS = 3
COMBINE_TM = 128
GATHER_UNROLL = 8
MOE_W_SLABS = 4
BULK_DMA_PRIORITY = 1


def _params(sem):
    return pltpu.CompilerParams(dimension_semantics=sem, vmem_limit_bytes=VMEM_LIMIT_BYTES)


def _dot(a, b):
    return jnp.dot(a, b, preferred_element_type=F32)


def _dot_nt(a, b):
    return lax.dot_general(a, b, (((1,), (1,)), ((), ())), preferred_element_type=F32)


def _dot_tn(a, b):
    return lax.dot_general(a, b, (((0,), (0,)), ((), ())), preferred_element_type=F32)


def _split3(x):
    hi = x.astype(BF16)
    r1 = x - hi.astype(F32)
    mid = r1.astype(BF16)
    lo = (r1 - mid.astype(F32)).astype(BF16)
    return hi, mid, lo


def _silu(x):
    return x * (1.0 / (1.0 + jnp.exp(-x)))


def _row_segments(operands, tile):
    stacks = [op if isinstance(op, (tuple, list)) else (op,) for op in operands]
    cuts = set()
    for parts in stacks:
        edge = 0
        for p in parts:
            assert p.shape[0] % tile == 0
            cuts.add(edge)
            edge += p.shape[0] // tile
        cuts.add(edge)
    cuts = sorted(cuts)
    segments = [(a, b - a) for a, b in zip(cuts[:-1], cuts[1:])]
    sources = []
    for parts in stacks:
        assert sum(p.shape[0] for p in parts) == cuts[-1] * tile
        src, edge = [], 0
        for p in parts:
            for first, _ in segments:
                if edge <= first < edge + p.shape[0] // tile:
                    src.append((p, first - edge))
            edge += p.shape[0] // tile
        sources.append(src)
    return segments, sources


def _segment_spec(block, segment, source_tile0, row_axis, col_of):
    first, n_tiles = segment

    def index_map(*ids):
        return (source_tile0 + jnp.clip(ids[row_axis] - first, 0, n_tiles - 1), col_of(*ids))
    return pl.BlockSpec(block, index_map)


def _for_current_segment(segments, tile_id, fn):
    if len(segments) == 1:
        fn(0)
        return
    for k, (first, n_tiles) in enumerate(segments):
        pl.when(jnp.logical_and(tile_id >= first, tile_id < first + n_tiles))(functools.partial(fn, k))


def _rmsnorm_kernel(*refs, segments):
    x_refs, (w_ref, o_ref) = refs[:len(segments)], refs[len(segments):]

    def run(k):
        x = x_refs[k][...]
        ms = jnp.mean(x * x, axis=-1, keepdims=True)
        o_ref[...] = (x * lax.rsqrt(ms + EPS) * w_ref[...]).astype(o_ref.dtype)

    _for_current_segment(segments, pl.program_id(0), run)


def _rmsnorm(x, w, out_dtype):
    segments, (src,) = _row_segments([x], ROW_TILE)
    d = src[0][0].shape[1]
    n_tiles = segments[-1][0] + segments[-1][1]
    return pl.pallas_call(
        functools.partial(_rmsnorm_kernel, segments=segments),
        grid=(n_tiles,),
        in_specs=[_segment_spec((ROW_TILE, d), seg, t0, 0, lambda i: 0)
                  for seg, (_, t0) in zip(segments, src)]
        + [pl.BlockSpec((1, d), lambda i: (0, 0))],
        out_specs=pl.BlockSpec((ROW_TILE, d), lambda i: (i, 0)),
        out_shape=jax.ShapeDtypeStruct((n_tiles * ROW_TILE, d), out_dtype),
        compiler_params=_params(("arbitrary",)),
        name="rmsnorm",
    )(*[p for p, _ in src], w.reshape(1, d))


def _rmsnorm_rows(x, w, out_dtype, row0, n_rows):
    d = x.shape[1]
    assert row0 % ROW_TILE == 0 and n_rows % ROW_TILE == 0
    seg = (0, n_rows // ROW_TILE)
    return pl.pallas_call(
        functools.partial(_rmsnorm_kernel, segments=[seg]),
        grid=(n_rows // ROW_TILE,),
        in_specs=[_segment_spec((ROW_TILE, d), seg, row0 // ROW_TILE, 0, lambda i: 0),
                  pl.BlockSpec((1, d), lambda i: (0, 0))],
        out_specs=pl.BlockSpec((ROW_TILE, d), lambda i: (i, 0)),
        out_shape=jax.ShapeDtypeStruct((n_rows, d), out_dtype),
        compiler_params=_params(("arbitrary",)),
        name="rmsnorm",
    )(x, w.reshape(1, d))


def _matmul_kernel(*refs, segments, has_res, w_is_nk):
    n = len(segments)
    a_refs, w_ref = refs[:n], refs[n]
    r_refs = refs[n + 1:2 * n + 1] if has_res else None
    o_ref, wb_ref = refs[-2:]
    i = pl.program_id(1)

    @pl.when(i == 0)
    def _():
        wb_ref[...] = w_ref[...].astype(BF16)

    def run(k):
        acc = (_dot_nt if w_is_nk else _dot)(a_refs[k][...], wb_ref[...])
        o_ref[...] = acc + r_refs[k][...] if has_res else acc

    _for_current_segment(segments, i, run)


def _matmul(a, w, n_cols, tm, tn, res=None, layer=None, w_is_nk=False):
    segments, sources = _row_segments([a] if res is None else [a, res], tm)
    k = sources[0][0][0].shape[1]
    m_tiles = segments[-1][0] + segments[-1][1]
    in_specs = [_segment_spec((tm, k), seg, t0, 1, lambda j, i: 0)
                for seg, (_, t0) in zip(segments, sources[0])]
    if w_is_nk:
        in_specs.append(pl.BlockSpec((tn, k), lambda j, i: (j, 0)))
    elif layer is None:
        in_specs.append(pl.BlockSpec((k, tn), lambda j, i: (0, j)))
    else:
        in_specs.append(pl.BlockSpec((None, k, tn), lambda j, i: (layer, 0, j)))
    args = [p for p, _ in sources[0]] + [w]
    if res is not None:
        in_specs += [_segment_spec((tm, tn), seg, t0, 1, lambda j, i: j)
                     for seg, (_, t0) in zip(segments, sources[1])]
        args += [p for p, _ in sources[1]]
    return pl.pallas_call(
        functools.partial(_matmul_kernel, segments=segments, has_res=res is not None,
                          w_is_nk=w_is_nk),
        grid=(n_cols // tn, m_tiles),
        in_specs=in_specs,
        out_specs=pl.BlockSpec((tm, tn), lambda j, i: (i, j)),
        out_shape=jax.ShapeDtypeStruct((m_tiles * tm, n_cols), F32),
        scratch_shapes=[pltpu.VMEM((tn, k) if w_is_nk else (k, tn), BF16)],
        compiler_params=_params(("arbitrary", "arbitrary")),
        name="matmul",
    )(*args)


def _ret_kernel(lg_ref, q_ref, k_ref, v_ref, g_ref, cos_ref, sin_ref, gn_ref, *rest,
                chunk, n_seq, chunks_per_seq, has_s0):
    if has_s0:
        s0a_ref, s0b_ref, o_ref, s_ref = rest
    else:
        o_ref, s_ref = rest
    c = chunk
    h = pl.program_id(1)
    lg = lg_ref[h]

    if not has_s0:
        @pl.when(pl.program_id(2) == 0)
        def _():
            s_ref[...] = jnp.zeros(s_ref.shape, F32)

    ii = lax.broadcasted_iota(jnp.int32, (c, 1), 0).astype(F32)
    jj = lax.broadcasted_iota(jnp.int32, (1, c), 1).astype(F32)
    diff = ii - jj
    causal = diff >= 0.0
    decay = jnp.where(causal, jnp.exp(lg * jnp.where(causal, diff, 0.0)), 0.0)
    q_decay = jnp.exp(lg * (ii + 1.0))
    k_decay = jnp.exp(lg * (c - 1.0 - ii))
    chunk_decay = jnp.exp(jnp.full((1, 1), lg * c, F32))
    half = RET_DK // 2
    k_scale = RET_DK ** -0.5
    gn_w = gn_ref[...]
    op = (lambda t: t.astype(BF16)) if c % BF16_ROWS == 0 else (lambda t: t)
    outs = []

    for s in range(n_seq):
        state = jnp.concatenate([s0a_ref[s], s0b_ref[s]], axis=0) if has_s0 else s_ref[s]
        for ci in range(chunks_per_seq):
            r0 = (s * chunks_per_seq + ci) * c
            rows = slice(r0, r0 + c)
            cos = cos_ref[rows, :]
            sin = sin_ref[rows, :]
            q = q_ref[rows, :]
            k = k_ref[rows, :]
            q1, q2 = q[:, :half], q[:, half:]
            k1, k2 = k[:, :half], k[:, half:]
            qr = jnp.concatenate([q1 * cos - q2 * sin, q1 * sin + q2 * cos], axis=-1)
            kr = jnp.concatenate([k1 * cos - k2 * sin, k1 * sin + k2 * cos], axis=-1) * k_scale
            vb = op(v_ref[rows, :])
            qb = op(qr)
            scores = _dot_nt(qb, op(kr)) * decay
            inner = _dot(op(scores), vb)
            cross = _dot(qb, op(state)) * q_decay
            o = inner + cross
            state = state * chunk_decay + _dot_tn(op(kr * k_decay), vb)
            mu = jnp.mean(o, axis=-1, keepdims=True)
            oc = o - mu
            var = jnp.mean(oc * oc, axis=-1, keepdims=True)
            on = oc * lax.rsqrt(var + EPS) * gn_w
            res = on * _silu(g_ref[rows, :])
            if c % BF16_ROWS == 0:
                o_ref[rows, :] = res.astype(o_ref.dtype)
            else:
                outs.append(res)
        s_ref[s] = state
    if outs:
        o_ref[...] = jnp.concatenate(outs, axis=0).astype(o_ref.dtype)


def _retention(proj, row0, n_seq_total, seq_len, pos0, s0, gn_w, *, chunk, tb, bb):
    hq = RET_HEADS * RET_DK
    n_rows = n_seq_total * seq_len
    blk_rows = bb * tb
    n_inner = seq_len // tb
    n_outer = n_seq_total // bb
    assert row0 % blk_rows == 0 and seq_len % tb == 0 and tb % chunk == 0
    assert s0 is None or n_inner == 1
    rb0 = row0 // blk_rows
    half = RET_DK // 2

    pos = (pos0 + jnp.arange(seq_len)).astype(F32)
    inv = ROPE_BASE ** (-jnp.arange(half, dtype=F32) / half)
    ang = pos[:, None] * inv[None, :]
    cos = jnp.tile(jnp.cos(ang), (bb, 1)) if n_inner == 1 else jnp.cos(ang)
    sin = jnp.tile(jnp.sin(ang), (bb, 1)) if n_inner == 1 else jnp.sin(ang)
    log_gamma = jnp.log1p(-jnp.exp2(-5.0 - jnp.arange(RET_HEADS, dtype=F32)))

    def row_map(b, h, n):
        return rb0 + b * n_inner + n

    in_specs = [
        pl.BlockSpec(memory_space=pltpu.SMEM),
        pl.BlockSpec((blk_rows, RET_DK), lambda b, h, n: (row_map(b, h, n), h)),
        pl.BlockSpec((blk_rows, RET_DK), lambda b, h, n: (row_map(b, h, n), RET_HEADS + h)),
        pl.BlockSpec((blk_rows, RET_DV), lambda b, h, n: (row_map(b, h, n), 2 * hq // RET_DV + h)),
        pl.BlockSpec((blk_rows, RET_DV), lambda b, h, n: (row_map(b, h, n), 2 * hq // RET_DV + RET_HEADS + h)),
        pl.BlockSpec((blk_rows, half), lambda b, h, n: (n, 0)),
        pl.BlockSpec((blk_rows, half), lambda b, h, n: (n, 0)),
        pl.BlockSpec((None, 1, RET_DV), lambda b, h, n: (h, 0, 0)),
    ]
    args = [log_gamma, proj, proj, proj, proj, cos, sin, gn_w.reshape(RET_HEADS, 1, RET_DV)]
    if s0 is not None:
        in_specs += [pl.BlockSpec((bb, None, RET_DK // 2, RET_DV), lambda b, h, n: (b, h, 0, 0)),
                     pl.BlockSpec((bb, None, RET_DK // 2, RET_DV), lambda b, h, n: (b, h, 1, 0))]
        args += [s0, s0]
    out, s_fin = pl.pallas_call(
        functools.partial(_ret_kernel, chunk=chunk, n_seq=bb, chunks_per_seq=tb // chunk,
                          has_s0=s0 is not None),
        grid=(n_outer, RET_HEADS, n_inner),
        in_specs=in_specs,
        out_specs=[pl.BlockSpec((blk_rows, RET_DV), lambda b, h, n: (b * n_inner + n, h)),
                   pl.BlockSpec((bb, None, RET_DK, RET_DV), lambda b, h, n: (b, h, 0, 0))],
        out_shape=[jax.ShapeDtypeStruct((n_rows, RET_HEADS * RET_DV), BF16),
                   jax.ShapeDtypeStruct((n_seq_total, RET_HEADS, RET_DK, RET_DV), F32)],
        compiler_params=_params(("parallel", "parallel", "arbitrary")),
        name="retention",
    )(*args)
    return out, s_fin


def _shift_rows(x, prev8, s, x_ref=None):
    n = x.shape[0]
    first = x[:SUBLANES]
    row = lax.broadcasted_iota(jnp.int32, first.shape, 0)
    top = jnp.where(row < s, pltpu.roll(prev8, s, 0), pltpu.roll(first, s, 0))
    if n == SUBLANES:
        return top
    if x_ref is None:
        rest = pltpu.roll(x, s, 0)[SUBLANES:]
    else:
        rest = x_ref[pl.ds(SUBLANES - s, n - SUBLANES), :]
    return jnp.concatenate([top, rest], axis=0)


def _causal_conv_silu(x, prev8, cw, x_ref=None):
    acc = x * cw[CONV_W - 1:CONV_W, :]
    for s in range(1, CONV_W):
        acc = acc + _shift_rows(x, prev8, s, x_ref) * cw[CONV_W - 1 - s:CONV_W - s, :]
    return _silu(acc)


def _gdn_kernel(q_ref, k_ref, v_ref, z_ref, gb_ref, alog_ref, dtb_ref, cq_ref, ck_ref, cv_ref,
                nw_ref, *rest, chunk, group, n_seq, rows_per_seq, carry):
    has_init = not carry
    if has_init:
        pq_ref, pk_ref, pv_ref, s0_ref, o_ref, s_ref, prev_ref = rest
    else:
        o_ref, s_ref, prev_ref = rest
    c = chunk
    r = n_seq * rows_per_seq
    m = group
    dk = GDN_DK
    op = (lambda t: t.astype(BF16)) if c % BF16_ROWS == 0 else (lambda t: t)

    if carry:
        @pl.when(pl.program_id(2) == 0)
        def _():
            s_ref[...] = jnp.zeros(s_ref.shape, F32)
            prev_ref[...] = jnp.zeros(prev_ref.shape, F32)

    q_raw, k_raw, v_raw = q_ref[...], k_ref[...], v_ref[...]
    if carry:
        prev = prev_ref[...]
        qc = _causal_conv_silu(q_raw, prev[:, :dk], cq_ref[...], q_ref)
        kc = _causal_conv_silu(k_raw, prev[:, dk:2 * dk], ck_ref[...], k_ref)
        vc = _causal_conv_silu(v_raw, prev[:, 2 * dk:], cv_ref[...], v_ref)
        prev_ref[...] = jnp.concatenate(
            [q_raw[r - SUBLANES:], k_raw[r - SUBLANES:], v_raw[r - SUBLANES:]], axis=-1)
    else:
        qs, ks, vs = [], [], []
        for s in range(n_seq):
            rows = slice(s * rows_per_seq, (s + 1) * rows_per_seq)
            qs.append(_causal_conv_silu(q_raw[rows], pq_ref[s], cq_ref[...]))
            ks.append(_causal_conv_silu(k_raw[rows], pk_ref[s], ck_ref[...]))
            vs.append(_causal_conv_silu(v_raw[rows], pv_ref[s], cv_ref[...]))
        qc = jnp.concatenate(qs, axis=0)
        kc = jnp.concatenate(ks, axis=0)
        vc = jnp.concatenate(vs, axis=0)

    qn = qc * lax.rsqrt(jnp.sum(qc * qc, axis=-1, keepdims=True) + EPS) * (dk ** -0.5)
    kn = kc * lax.rsqrt(jnp.sum(kc * kc, axis=-1, keepdims=True) + EPS)
    qb = qn.astype(BF16)
    kb = kn.astype(BF16)

    gb = gb_ref[...]
    a_coef = jnp.exp(alog_ref[...][:, :1])
    xs = gb + dtb_ref[...][:, :1]
    softplus = jnp.maximum(xs, 0.0) + jnp.log(1.0 + jnp.exp(-jnp.abs(xs)))
    g_rows = -a_coef * softplus
    beta_rows = 1.0 / (1.0 + jnp.exp(-gb))
    rid = lax.broadcasted_iota(jnp.int32, gb.shape, 0)
    x8 = jnp.where(rid < 2, g_rows, beta_rows)

    ri = lax.broadcasted_iota(jnp.int32, (m, m), 0)
    ci = lax.broadcasted_iota(jnp.int32, (m, m), 1)
    log2c = int(math.log2(c))
    same = lax.shift_right_logical(ri, log2c) == lax.shift_right_logical(ci, log2c)
    tri_incl = jnp.logical_and(same, ri >= ci)
    tri_strict = jnp.logical_and(same, ri > ci)
    tri_b = jnp.where(tri_incl, 1.0, 0.0).astype(BF16)
    eye_b = jnp.where(ri == ci, 1.0, 0.0).astype(BF16)
    eye_f = jnp.where(ri == ci, 1.0, 0.0)
    n_sq = max(log2c - 1, 0)
    nw = nw_ref[...]

    units = [(g, vh) for g in range(r // m) for vh in range(2)]
    rows_of = {g: slice(g * m, (g + 1) * m) for g in range(r // m)}
    gate = {}
    for g in range(r // m):
        rows_g = rows_of[g]
        parts = _split3(x8[:, rows_g])
        cs_col = sum(_dot_nt(tri_b, p) for p in parts)
        cs_row = sum(_dot_nt(p, tri_b) for p in parts)
        x_col = sum(_dot_nt(eye_b, p) for p in parts)
        gate[g] = (cs_col, cs_row, x_col, jnp.exp(cs_col),
                   _dot_nt(kb[rows_g], kb[rows_g]), _dot_nt(qb[rows_g], kb[rows_g]))

    g_col, b_col, eg_col, dmat, t_mat, pw = {}, {}, {}, {}, {}, {}
    for u in units:
        g, vh = u
        cs_col, cs_row, x_col, exp_col, kk, _ = gate[g]
        g_col[u] = cs_col[:, vh:vh + 1]
        b_col[u] = x_col[:, 2 + vh:3 + vh]
        eg_col[u] = exp_col[:, vh:vh + 1]
        dmat[u] = jnp.exp(jnp.where(tri_incl, g_col[u] - cs_row[vh:vh + 1, :], -jnp.inf))
        pw[u] = jnp.where(tri_strict, -(kk * dmat[u]) * b_col[u], 0.0)
        t_mat[u] = eye_f + pw[u]
    for _ in range(n_sq):
        for u in units:
            pwb = pw[u].astype(BF16)
            pw[u] = _dot(pwb, pwb)
        for u in units:
            t_mat[u] = t_mat[u] + _dot(t_mat[u].astype(BF16), pw[u].astype(BF16))

    sol, q_in = {}, {}
    for u in units:
        g, vh = u
        v_h = vc[rows_of[g], vh * GDN_DV:(vh + 1) * GDN_DV]
        rhs = jnp.concatenate([v_h * b_col[u], kn[rows_of[g]] * (b_col[u] * eg_col[u])], axis=-1)
        sol[u] = _dot(t_mat[u].astype(BF16), rhs.astype(BF16))
        q_in[u] = qn[rows_of[g]] * eg_col[u]

    chunks = [(u, ch) for u in units for ch in range(m // c)]
    k_dec, g_last, qp = {}, {}, {}
    for uc in chunks:
        u, ch = uc
        rows = slice(ch * c, (ch + 1) * c)
        g_last[uc] = g_col[u][(ch + 1) * c - 1:(ch + 1) * c, :]
        k_dec[uc] = kn[rows_of[u[0]]][rows] * jnp.exp(g_last[uc] - g_col[u][rows])
        if carry:
            qp[uc] = _dot_tn(k_dec[uc].astype(BF16), sol[u][rows].astype(BF16))

    v_new, o_inter = {}, {}
    if carry:
        states = [s_ref[0, vh] for vh in range(2)]
        sb = {}
        for g in range(r // m):
            for ch in range(m // c):
                for vh in range(2):
                    uc = ((g, vh), ch)
                    sb[uc] = states[vh].astype(BF16)
                    states[vh] = (states[vh] * jnp.exp(g_last[uc])
                                  - _dot(qp[uc][:, GDN_DV:].astype(BF16), sb[uc]) + qp[uc][:, :GDN_DV])
        for vh in range(2):
            s_ref[0, vh] = states[vh]
        for uc in chunks:
            u, ch = uc
            rows = slice(ch * c, (ch + 1) * c)
            both = _dot(jnp.concatenate([sol[u][rows, GDN_DV:].astype(BF16),
                                         q_in[u][rows].astype(BF16)], axis=0), sb[uc])
            v_new[uc] = sol[u][rows, :GDN_DV] - both[:c]
            o_inter[uc] = both[c:]
    else:
        for uc in chunks:
            u, ch = uc
            rows = slice(ch * c, (ch + 1) * c)
            seq = (u[0] * m + ch * c) // rows_per_seq
            state = s0_ref[seq, u[1]]
            both = _dot(jnp.concatenate([sol[u][rows, GDN_DV:], q_in[u][rows]], axis=0).astype(BF16),
                        state.astype(BF16))
            v_new[uc] = sol[u][rows, :GDN_DV] - both[:c]
            o_inter[uc] = both[c:]
            s_ref[seq, u[1]] = (state * jnp.exp(g_last[uc])
                                + _dot_tn(op(k_dec[uc]), op(v_new[uc])))

    for u in units:
        g, vh = u
        v_new_all = jnp.concatenate([v_new[(u, ch)] for ch in range(m // c)], axis=0)
        o = jnp.concatenate([o_inter[(u, ch)] for ch in range(m // c)], axis=0)
        attn = jnp.where(tri_incl, gate[g][5] * dmat[u], 0.0)
        o = o + _dot(attn.astype(BF16), v_new_all.astype(BF16))
        o = o * lax.rsqrt(jnp.mean(o * o, axis=-1, keepdims=True) + EPS) * nw
        z = z_ref[rows_of[g], vh * GDN_DV:(vh + 1) * GDN_DV]
        o_ref[rows_of[g], vh * GDN_DV:(vh + 1) * GDN_DV] = (o * _silu(z)).astype(o_ref.dtype)


def _gdn(proj, gates, row0, n_seq_total, seq_len, conv0_pad, s0, conv_w, a_log, dt_bias, norm_w,
         *, chunk, tb, bb):
    hk = GDN_K_HEADS * GDN_DK
    hv = GDN_V_HEADS * GDN_DV
    n_rows = n_seq_total * seq_len
    blk_rows = bb * tb
    n_inner = seq_len // tb
    n_outer = n_seq_total // bb
    carry = s0 is None
    assert row0 % blk_rows == 0 and seq_len % tb == 0 and tb % chunk == 0
    assert carry == (bb == 1) and (carry or (n_inner == 1 and tb == chunk))
    assert blk_rows % GDN_GROUP == 0 and GDN_GROUP % chunk == 0
    rb0 = row0 // blk_rows
    two = 2 * GDN_DV

    def row_map(b, h, n):
        return rb0 + b * n_inner + n

    alog8 = jnp.zeros((GDN_K_HEADS, SUBLANES, LANES), F32).at[:, :2, :].set(
        jnp.broadcast_to(a_log.reshape(GDN_K_HEADS, 2, 1), (GDN_K_HEADS, 2, LANES)))
    dtb8 = jnp.zeros((GDN_K_HEADS, SUBLANES, LANES), F32).at[:, :2, :].set(
        jnp.broadcast_to(dt_bias.reshape(GDN_K_HEADS, 2, 1), (GDN_K_HEADS, 2, LANES)))

    in_specs = [
        pl.BlockSpec((blk_rows, GDN_DK), lambda b, h, n: (row_map(b, h, n), h)),
        pl.BlockSpec((blk_rows, GDN_DK), lambda b, h, n: (row_map(b, h, n), GDN_K_HEADS + h)),
        pl.BlockSpec((blk_rows, two), lambda b, h, n: (row_map(b, h, n), 2 * hk // two + h)),
        pl.BlockSpec((blk_rows, two), lambda b, h, n: (row_map(b, h, n), CONV_DIM // two + h)),
        pl.BlockSpec((None, SUBLANES, blk_rows), lambda b, h, n: (h, 0, row_map(b, h, n))),
        pl.BlockSpec((None, SUBLANES, LANES), lambda b, h, n: (h, 0, 0)),
        pl.BlockSpec((None, SUBLANES, LANES), lambda b, h, n: (h, 0, 0)),
        pl.BlockSpec((CONV_W, GDN_DK), lambda b, h, n: (0, h)),
        pl.BlockSpec((CONV_W, GDN_DK), lambda b, h, n: (0, GDN_K_HEADS + h)),
        pl.BlockSpec((CONV_W, two), lambda b, h, n: (0, 2 * hk // two + h)),
        pl.BlockSpec((1, GDN_DV), lambda b, h, n: (0, 0)),
    ]
    args = [proj, proj, proj, proj, gates, alog8, dtb8, conv_w, conv_w, conv_w,
            norm_w.reshape(1, GDN_DV)]
    if not carry:
        in_specs += [
            pl.BlockSpec((bb, SUBLANES, GDN_DK), lambda b, h, n: (b, 0, h)),
            pl.BlockSpec((bb, SUBLANES, GDN_DK), lambda b, h, n: (b, 0, GDN_K_HEADS + h)),
            pl.BlockSpec((bb, SUBLANES, two), lambda b, h, n: (b, 0, 2 * hk // two + h)),
            pl.BlockSpec((bb, 2, GDN_DK, GDN_DV), lambda b, h, n: (b, h, 0, 0)),
        ]
        args += [conv0_pad, conv0_pad, conv0_pad, s0]
    out, s_fin = pl.pallas_call(
        functools.partial(_gdn_kernel, chunk=chunk, group=GDN_GROUP, n_seq=bb, rows_per_seq=tb,
                          carry=carry),
        grid=(n_outer, GDN_K_HEADS, n_inner),
        in_specs=in_specs,
        out_specs=[pl.BlockSpec((blk_rows, two), lambda b, h, n: (b * n_inner + n, h)),
                   pl.BlockSpec((bb, 2, GDN_DK, GDN_DV), lambda b, h, n: (b, h, 0, 0))],
        out_shape=[jax.ShapeDtypeStruct((n_rows, hv), BF16),
                   jax.ShapeDtypeStruct((n_seq_total, GDN_V_HEADS, GDN_DK, GDN_DV), F32)],
        scratch_shapes=[pltpu.VMEM((SUBLANES, 2 * GDN_DK + two), F32)],
        compiler_params=_params(("parallel", "parallel", "arbitrary")),
        name="gated_delta",
    )(*args)
    return out, s_fin


def _router_kernel(x_ref, w_ref, wr_ref, br_ref, h_ref, lg_ref):
    x = x_ref[...]
    ms = jnp.mean(x * x, axis=-1, keepdims=True)
    h = x * lax.rsqrt(ms + EPS) * w_ref[...]
    h_ref[...] = h
    lg_ref[...] = _dot_nt(wr_ref[...].astype(BF16), h.astype(BF16)) + br_ref[...]


def _router(x, norm_w, w_route, b_route):
    t, d = x.shape
    return pl.pallas_call(
        _router_kernel,
        grid=(t // ROW_TILE,),
        in_specs=[pl.BlockSpec((ROW_TILE, d), lambda i: (i, 0)),
                  pl.BlockSpec((1, d), lambda i: (0, 0)),
                  pl.BlockSpec((LANES, d), lambda i: (0, 0)),
                  pl.BlockSpec((LANES, 1), lambda i: (0, 0))],
        out_specs=[pl.BlockSpec((ROW_TILE, d), lambda i: (i, 0)),
                   pl.BlockSpec((LANES, ROW_TILE), lambda i: (0, i))],
        out_shape=[jax.ShapeDtypeStruct((t, d), F32),
                   jax.ShapeDtypeStruct((LANES, t), F32)],
        compiler_params=_params(("parallel",)),
        name="moe_router",
    )(x, norm_w.reshape(1, d), w_route, b_route)


def _row_gather_start(idx_ref, base, n, src_hbm, dst_ref, sem):
    def body(r, carry):
        tok = idx_ref[base + r]
        pltpu.make_async_copy(src_hbm.at[pl.ds(tok, 1)], dst_ref.at[pl.ds(r, 1)], sem).start()
        return carry
    lax.fori_loop(0, n, body, 0, unroll=GATHER_UNROLL)


def _row_gather_wait(n, src_hbm, dst_ref, sem):
    pltpu.make_async_copy(src_hbm.at[pl.ds(0, n)], dst_ref, sem).wait()


def _row_gather_start_counted(idx_ref, base, n, src_hbm, dst_ref, sem):
    def body(g, carry):
        for k in range(GATHER_UNROLL):
            r = g * GATHER_UNROLL + k
            tok = idx_ref[base + r]
            pltpu.make_async_copy(src_hbm.at[pl.ds(tok, 1)], dst_ref.at[pl.ds(r, 1)], sem).start()
        return carry
    lax.fori_loop(0, n // GATHER_UNROLL, body, 0)


def _row_gather_wait_counted(n, src_hbm, dst_ref, sem):
    @pl.when(n > 0)
    def _():
        rows = pl.multiple_of(n, GATHER_UNROLL)
        pltpu.make_async_copy(src_hbm.at[pl.ds(0, rows)], dst_ref.at[pl.ds(0, rows)], sem).wait()


def _moe_ffn_kernel(be_ref, first_ref, nxt_ref, ws_ref, cnt_ref, rt_ref, na_ref, h_hbm, rw_ref,
                    win_hbm, wout_hbm, o_ref, xbuf, sem, win_f, wout_f, win_sem, wout_sem, winb, woutb,
                    *, layer):
    i = pl.program_id(0)
    n_active = na_ref[0]
    slot = lax.rem(i, MOE_GATHER_SLOTS)

    def gather_start(blk):
        s = lax.rem(blk, MOE_GATHER_SLOTS)
        _row_gather_start_counted(rt_ref, blk * MOE_BLOCK, cnt_ref[blk], h_hbm, xbuf.at[s], sem.at[s])

    @pl.when(i == 0)
    def _():
        xbuf[...] = jnp.zeros(xbuf.shape, F32)

    def weight_copies(e, ws):
        r_in = win_f.shape[1] // MOE_W_SLABS
        r_out = wout_f.shape[1] // MOE_W_SLABS
        cps = []
        for k in range(MOE_W_SLABS):
            cps.append(pltpu.make_async_copy(win_hbm.at[layer, e, pl.ds(k * r_in, r_in)],
                                             win_f.at[ws, pl.ds(k * r_in, r_in)], win_sem.at[ws, k]))
            cps.append(pltpu.make_async_copy(wout_hbm.at[layer, e, pl.ds(k * r_out, r_out)],
                                             wout_f.at[ws, pl.ds(k * r_out, r_out)], wout_sem.at[ws, k]))
        return cps

    @pl.when(jnp.logical_and(i == 0, n_active > 0))
    def _():
        for cp in weight_copies(be_ref[0], ws_ref[0]):
            cp.start(priority=BULK_DMA_PRIORITY)
        for blk in range(MOE_GATHER_SLOTS - 1):
            pl.when(blk < n_active)(functools.partial(gather_start, blk))

    @pl.when(i + MOE_GATHER_SLOTS - 1 < n_active)
    def _():
        gather_start(i + MOE_GATHER_SLOTS - 1)

    @pl.when(i < n_active)
    def _():
        ws = ws_ref[i]

        @pl.when(first_ref[i] == 1)
        def _():
            @pl.when(nxt_ref[i] >= 0)
            def _():
                for cp in weight_copies(nxt_ref[i], 1 - ws):
                    cp.start(priority=BULK_DMA_PRIORITY)

            for cp in weight_copies(be_ref[i], ws):
                cp.wait()
            winb[...] = win_f[ws].astype(BF16)
            woutb[...] = wout_f[ws].astype(BF16)

        _row_gather_wait_counted(cnt_ref[i], h_hbm, xbuf.at[slot], sem.at[slot])
        x = xbuf[slot].astype(BF16)
        mid = _dot(x, winb[...])
        act = (_silu(mid[:, :D_EXPERT]) * mid[:, D_EXPERT:]).astype(BF16)
        o_ref[...] = _dot(act, woutb[...]) * rw_ref[...]

    @pl.when(i >= n_active)
    def _():
        o_ref[...] = jnp.zeros(o_ref.shape, F32)


def _moe_ffn(h, block_e, block_first, block_next, block_wslot, block_cnt, row_tok, n_active, row_w,
             w_in, w_out, layer):
    t, d = h.shape
    n_blocks = block_e.shape[0]
    n_rows = n_blocks * MOE_BLOCK
    grid_spec = pltpu.PrefetchScalarGridSpec(
        num_scalar_prefetch=7,
        grid=(n_blocks,),
        in_specs=[
            pl.BlockSpec(memory_space=pl.ANY),
            pl.BlockSpec((MOE_BLOCK, 1), lambda i, *_: (i, 0)),
            pl.BlockSpec(memory_space=pl.ANY),
            pl.BlockSpec(memory_space=pl.ANY),
        ],
        out_specs=pl.BlockSpec((MOE_BLOCK, d), lambda i, *_: (i, 0)),
        scratch_shapes=[pltpu.VMEM((MOE_GATHER_SLOTS, MOE_BLOCK, d), F32),
                        pltpu.SemaphoreType.DMA((MOE_GATHER_SLOTS,)),
                        pltpu.VMEM((2, d, 2 * D_EXPERT), F32),
                        pltpu.VMEM((2, D_EXPERT, d), F32),
                        pltpu.SemaphoreType.DMA((2, MOE_W_SLABS)),
                        pltpu.SemaphoreType.DMA((2, MOE_W_SLABS)),
                        pltpu.VMEM((d, 2 * D_EXPERT), BF16),
                        pltpu.VMEM((D_EXPERT, d), BF16)],
    )
    return pl.pallas_call(
        functools.partial(_moe_ffn_kernel, layer=layer),
        grid_spec=grid_spec,
        out_shape=jax.ShapeDtypeStruct((n_rows, d), F32),
        compiler_params=_params(("arbitrary",)),
        name="moe_ffn",
    )(block_e, block_first, block_next, block_wslot, block_cnt, row_tok, n_active, h,
      row_w.reshape(n_rows, 1), w_in, w_out)


def _combine_kernel(pos_ref, ys_hbm, x_ref, *rest, with_norm):
    if with_norm:
        nw_ref, o_ref, hn_ref, buf, sem = rest
    else:
        o_ref, buf, sem = rest
    i = pl.program_id(0)
    n = pl.num_programs(0)
    slot = lax.rem(i, 2)
    rows = 2 * COMBINE_TM

    @pl.when(i == 0)
    def _():
        _row_gather_start(pos_ref, 0, rows, ys_hbm, buf.at[0], sem.at[0])

    @pl.when(i + 1 < n)
    def _():
        _row_gather_start(pos_ref, (i + 1) * rows, rows, ys_hbm, buf.at[1 - slot], sem.at[1 - slot])

    _row_gather_wait(rows, ys_hbm, buf.at[slot], sem.at[slot])
    y = x_ref[...] + buf[slot, :COMBINE_TM, :] + buf[slot, COMBINE_TM:, :]
    o_ref[...] = y
    if with_norm:
        ms = jnp.mean(y * y, axis=-1, keepdims=True)
        hn_ref[...] = (y * lax.rsqrt(ms + EPS) * nw_ref[...]).astype(hn_ref.dtype)


def _combine(x, ys, pos, next_norm_w=None):
    t, d = x.shape
    with_norm = next_norm_w is not None
    row_spec = pl.BlockSpec((COMBINE_TM, d), lambda i, p: (i, 0))
    in_specs = [pl.BlockSpec(memory_space=pl.ANY), row_spec]
    args = [pos, ys, x]
    out_specs, out_shape = row_spec, jax.ShapeDtypeStruct((t, d), F32)
    if with_norm:
        in_specs.append(pl.BlockSpec((1, d), lambda i, p: (0, 0)))
        args.append(next_norm_w.reshape(1, d))
        out_specs = [row_spec, row_spec]
        out_shape = [out_shape, jax.ShapeDtypeStruct((t, d), BF16)]
    grid_spec = pltpu.PrefetchScalarGridSpec(
        num_scalar_prefetch=1,
        grid=(t // COMBINE_TM,),
        in_specs=in_specs,
        out_specs=out_specs,
        scratch_shapes=[pltpu.VMEM((2, 2 * COMBINE_TM, d), F32),
                        pltpu.SemaphoreType.DMA((2,))],
    )
    return pl.pallas_call(
        functools.partial(_combine_kernel, with_norm=with_norm),
        grid_spec=grid_spec,
        out_shape=out_shape,
        compiler_params=_params(("arbitrary",)),
        name="moe_combine",
    )(*args)


def _hier_moe(x, norm_w, w_group, b_group, w_expert, b_expert, w_in, w_out, layer, next_norm_w=None):
    t, d = x.shape
    n_route = MOE_GROUPS + N_EXPERTS
    w_route = jnp.zeros((LANES, d), F32).at[:MOE_GROUPS].set(w_group.T).at[MOE_GROUPS:n_route].set(w_expert.T)
    b_route = jnp.zeros((LANES, 1), F32).at[:MOE_GROUPS, 0].set(b_group).at[MOE_GROUPS:n_route, 0].set(b_expert)
    h, logits = _router(x, norm_w, w_route, b_route)

    g_logits = logits[:MOE_GROUPS]
    g_max = jnp.max(g_logits, axis=0)
    g_sel = jnp.argmax(g_logits, axis=0).astype(jnp.int32)
    g_w = 1.0 / jnp.sum(jnp.exp(g_logits - g_max[None, :]), axis=0)
    e_all = logits[MOE_GROUPS:n_route].reshape(MOE_GROUPS, EXPERTS_PER_GROUP, t)
    grp = jnp.arange(MOE_GROUPS, dtype=jnp.int32)[:, None, None]
    e_logits = jnp.sum(jnp.where(grp == g_sel[None, None, :], e_all, 0.0), axis=0)
    slot_id = jnp.arange(EXPERTS_PER_GROUP, dtype=jnp.int32)[:, None]
    i1 = jnp.argmax(e_logits, axis=0).astype(jnp.int32)
    v1 = jnp.max(e_logits, axis=0)
    rest = jnp.where(slot_id == i1[None, :], -jnp.inf, e_logits)
    i2 = jnp.argmax(rest, axis=0).astype(jnp.int32)
    v2 = jnp.max(rest, axis=0)
    e21 = jnp.exp(v2 - v1)
    weights = jnp.stack([g_w / (1.0 + e21), g_w * e21 / (1.0 + e21)], axis=1)
    expert_id = jnp.stack([g_sel * EXPERTS_PER_GROUP + i1, g_sel * EXPERTS_PER_GROUP + i2], axis=1)

    n_assign = t * TOP_K
    flat_e = expert_id.reshape(n_assign)
    iota = jnp.arange(n_assign, dtype=jnp.int32)
    experts = jnp.arange(N_EXPERTS, dtype=jnp.int32)
    sorted_e, order, w_sorted = lax.sort((flat_e, iota, weights.reshape(n_assign)), num_keys=1,
                                         is_stable=True)
    counts = jnp.sum((flat_e[:, None] == experts[None, :]).astype(jnp.int32), axis=0)
    start = jnp.cumsum(counts) - counts
    nblk_e = (counts + MOE_BLOCK - 1) // MOE_BLOCK
    blk_end = jnp.cumsum(nblk_e)
    blk_start = blk_end - nblk_e

    def lookup(idx, table):
        return jnp.sum(jnp.where(idx[:, None] == experts[None, :], table[None, :], 0), axis=1)

    dest_sorted = iota + lookup(sorted_e, blk_start * MOE_BLOCK - start)
    _, dest = lax.sort((order, dest_sorted), num_keys=1)

    n_blocks = -(-n_assign // MOE_BLOCK) + N_EXPERTS
    n_active = blk_end[-1].astype(jnp.int32)
    blk = jnp.minimum(jnp.arange(n_blocks, dtype=jnp.int32), jnp.maximum(n_active - 1, 0))
    block_e = jnp.minimum(jnp.sum((blk[:, None] >= blk_end[None, :]).astype(jnp.int32), axis=1),
                          N_EXPERTS - 1)
    block_first = (blk == lookup(block_e, blk_start)).astype(jnp.int32)
    active = counts > 0
    ordinal = jnp.cumsum(active.astype(jnp.int32)) - 1
    block_wslot = lookup(block_e, ordinal) % 2
    later = lax.cummin(jnp.where(active, experts, N_EXPERTS), reverse=True)
    next_e = jnp.concatenate([later[1:], jnp.full((1,), N_EXPERTS, jnp.int32)])
    block_next = lookup(block_e, jnp.where(next_e >= N_EXPERTS, -1, next_e))

    j_base = lookup(block_e, start) + (blk - lookup(block_e, blk_start)) * MOE_BLOCK
    j_end = lookup(block_e, start + counts)
    j = j_base[:, None] + jnp.arange(MOE_BLOCK, dtype=jnp.int32)[None, :]
    valid = jnp.logical_and(j < j_end[:, None],
                            jnp.arange(n_blocks, dtype=jnp.int32)[:, None] < n_active)
    j = jnp.clip(j, 0, n_assign - 1).reshape(-1)
    valid = valid.reshape(-1)
    row_tok = jnp.where(valid, order[j] // TOP_K, 0)
    row_w = jnp.where(valid, w_sorted[j], 0.0)
    block_cnt = jnp.where(jnp.arange(n_blocks, dtype=jnp.int32) < n_active,
                          jnp.clip(j_end - j_base, 0, MOE_BLOCK), 0)
    block_cnt = ((block_cnt + GATHER_UNROLL - 1) // GATHER_UNROLL) * GATHER_UNROLL

    ys = _moe_ffn(h, block_e, block_first, block_next, block_wslot, block_cnt.astype(jnp.int32), row_tok,
                  n_active.reshape(1), row_w, w_in, w_out, layer)
    pos = dest.reshape(t // COMBINE_TM, COMBINE_TM, TOP_K).transpose(0, 2, 1).reshape(n_assign)
    return _combine(x, ys, pos, next_norm_w)


def kernel(x_prompt, x_sample, state_ret, state_gdn, state_conv, norm_mix, norm_ffn, norm_final,
           ret_w_in, ret_gn, ret_w_out, gdn_w_in, gdn_conv, gdn_a_log, gdn_dt_bias, gdn_norm,
           gdn_w_out, moe_w_group, moe_b_group, moe_w_expert, moe_b_expert, moe_w_in, moe_w_out):
    bp, lp, d = x_prompt.shape
    bs, ls, _ = x_sample.shape
    tp, ts = bp * lp, bs * ls
    x = (x_prompt.reshape(tp, d), x_sample.reshape(ts, d))

    h = _rmsnorm(x, norm_mix[0], BF16)
    ret_in = ret_w_in.shape[-1]
    proj = _matmul(h, ret_w_in, ret_in, MM_TM, MM_TN, layer=0)
    o_p, ret_p = _retention(proj, 0, bp, lp, 0, None, ret_gn[0],
                            chunk=RET_CHUNK, tb=RET_TB, bb=1)
    o_s, ret_s = _retention(proj, tp, bs, ls, PAST_LEN, state_ret[0], ret_gn[0],
                            chunk=ls, tb=ls, bb=RET_SAMPLE_BB)
    x = _matmul((o_p, o_s), ret_w_out, d, ROW_TILE, MM_TN_OUT, res=x, layer=0)
    x, h = _hier_moe(x, norm_ffn[0], moe_w_group[0], moe_b_group[0], moe_w_expert[0], moe_b_expert[0],
                     moe_w_in, moe_w_out, 0, next_norm_w=norm_mix[1])

    hv = GDN_V_HEADS * GDN_DV
    n_main = CONV_DIM + hv
    proj = _matmul(h, jnp.transpose(gdn_w_in[0]), n_main, MM_TM, MM_TN, w_is_nk=True)
    w_tail = jnp.zeros((d, LANES), F32).at[:, :2 * GDN_V_HEADS].set(gdn_w_in[0, :, n_main:])
    tail = _matmul(h, w_tail, LANES, MM_TM, LANES)
    t = tp + ts
    bt = tail[:, :GDN_V_HEADS].reshape(t, GDN_K_HEADS, 2)
    a = tail[:, GDN_V_HEADS:2 * GDN_V_HEADS].reshape(t, GDN_K_HEADS, 2)
    gates = jnp.concatenate([a, bt, jnp.zeros((t, GDN_K_HEADS, 4), F32)], axis=-1).transpose(1, 2, 0)
    conv0_pad = jnp.pad(state_conv[0], ((0, 0), (SUBLANES - (CONV_W - 1), 0), (0, 0)))
    o_p, gdn_p = _gdn(proj, gates, 0, bp, lp, None, None, gdn_conv[0], gdn_a_log[0], gdn_dt_bias[0],
                      gdn_norm[0], chunk=GDN_CHUNK, tb=GDN_TB, bb=1)
    o_s, gdn_s = _gdn(proj, gates, tp, bs, ls, conv0_pad, state_gdn[0], gdn_conv[0], gdn_a_log[0],
                      gdn_dt_bias[0], gdn_norm[0], chunk=ls, tb=ls, bb=GDN_SAMPLE_BB)
    proj8 = proj.reshape(t // SUBLANES, SUBLANES, n_main)
    keep = SUBLANES - (CONV_W - 1)
    conv_p = jnp.concatenate(
        [lax.slice(proj8, ((b * lp + lp) // SUBLANES - 1, keep, 0),
                   ((b * lp + lp) // SUBLANES, SUBLANES, CONV_DIM)) for b in range(bp)], axis=0)
    conv_s = lax.slice(proj8, ((tp + ls) // SUBLANES - 1, keep, 0), (t // SUBLANES, SUBLANES, CONV_DIM),
                       (ls // SUBLANES, 1, 1))
    x = _matmul((o_p, o_s), gdn_w_out, d, ROW_TILE, MM_TN_OUT, res=x, layer=0)
    x = _hier_moe(x, norm_ffn[1], moe_w_group[1], moe_b_group[1], moe_w_expert[1], moe_b_expert[1],
                  moe_w_in, moe_w_out, 1)

    y_p = _rmsnorm_rows(x, norm_final, F32, 0, tp)
    y_s = _rmsnorm_rows(x, norm_final, F32, tp, ts)
    return (y_p.reshape(bp, lp, d), y_s.reshape(bs, ls, d),
            ret_p[None], ret_s[None], gdn_p[None], gdn_s[None], conv_p[None], conv_s[None])
```

```python
import functools
import math

import numpy as np
import jax
import jax.numpy as jnp
from jax import lax
from jax.experimental import pallas as pl
from jax.experimental.pallas import tpu as pltpu

F32 = jnp.float32
BF16 = jnp.bfloat16

D_MODEL = 2048
PAST_LEN = 16384
RET_HEADS = 8
RET_DK = 256
RET_DV = 512
ROPE_BASE = 10000.0
GDN_K_HEADS = 16
GDN_V_HEADS = 32
GDN_DK = 128
GDN_DV = 128
CONV_W = 4
CONV_DIM = 2 * GDN_K_HEADS * GDN_DK + GDN_V_HEADS * GDN_DV
MOE_GROUPS = 8
EXPERTS_PER_GROUP = 8
N_EXPERTS = MOE_GROUPS * EXPERTS_PER_GROUP
TOP_K = 2
D_EXPERT = D_MODEL // 4
EPS = 1e-6

LANES = 128
SUBLANES = 8
BF16_ROWS = 16
VMEM_LIMIT_BYTES = 56 * 1024 * 1024

ROW_TILE = 512
MM_TM = 1024
MM_TN = 1024
MM_TN_OUT = 512
RET_CHUNK = 128
RET_TB = 2048
RET_SAMPLE_BB = 8
GDN_CHUNK = 64
GDN_TB = 1024
GDN_GROUP = 128
GDN_SAMPLE_BB = 32
MOE_BLOCK = 128
MOE_GATHER_SLOTS = 3
COMBINE_TM = 128
COMBINE_SLOTS = 3
GATHER_UNROLL = 8
MOE_W_SLABS = 4
BULK_DMA_PRIORITY = 1


def _params(sem):
    return pltpu.CompilerParams(dimension_semantics=sem, vmem_limit_bytes=VMEM_LIMIT_BYTES)


def _dot(a, b):
    return jnp.dot(a, b, preferred_element_type=F32)


def _dot_nt(a, b):
    return lax.dot_general(a, b, (((1,), (1,)), ((), ())), preferred_element_type=F32)


def _dot_tn(a, b):
    return lax.dot_general(a, b, (((0,), (0,)), ((), ())), preferred_element_type=F32)


def _split3(x):
    hi = x.astype(BF16)
    r1 = x - hi.astype(F32)
    mid = r1.astype(BF16)
    lo = (r1 - mid.astype(F32)).astype(BF16)
    return hi, mid, lo


def _silu(x):
    return x * (1.0 / (1.0 + jnp.exp(-x)))


def _row_segments(operands, tile):
    stacks = [op if isinstance(op, (tuple, list)) else (op,) for op in operands]
    cuts = set()
    for parts in stacks:
        edge = 0
        for p in parts:
            assert p.shape[0] % tile == 0
            cuts.add(edge)
            edge += p.shape[0] // tile
        cuts.add(edge)
    cuts = sorted(cuts)
    segments = [(a, b - a) for a, b in zip(cuts[:-1], cuts[1:])]
    sources = []
    for parts in stacks:
        assert sum(p.shape[0] for p in parts) == cuts[-1] * tile
        src, edge = [], 0
        for p in parts:
            for first, _ in segments:
                if edge <= first < edge + p.shape[0] // tile:
                    src.append((p, first - edge))
            edge += p.shape[0] // tile
        sources.append(src)
    return segments, sources


def _segment_spec(block, segment, source_tile0, row_axis, col_of):
    first, n_tiles = segment

    def index_map(*ids):
        return (source_tile0 + jnp.clip(ids[row_axis] - first, 0, n_tiles - 1), col_of(*ids))
    return pl.BlockSpec(block, index_map)


def _for_current_segment(segments, tile_id, fn):
    if len(segments) == 1:
        fn(0)
        return
    for k, (first, n_tiles) in enumerate(segments):
        pl.when(jnp.logical_and(tile_id >= first, tile_id < first + n_tiles))(functools.partial(fn, k))


def _rmsnorm_kernel(*refs, segments):
    x_refs, (w_ref, o_ref) = refs[:len(segments)], refs[len(segments):]

    def run(k):
        x = x_refs[k][...]
        ms = jnp.mean(x * x, axis=-1, keepdims=True)
        o_ref[...] = (x * lax.rsqrt(ms + EPS) * w_ref[...]).astype(o_ref.dtype)

    _for_current_segment(segments, pl.program_id(0), run)


def _rmsnorm(x, w, out_dtype):
    segments, (src,) = _row_segments([x], ROW_TILE)
    d = src[0][0].shape[1]
    n_tiles = segments[-1][0] + segments[-1][1]
    return pl.pallas_call(
        functools.partial(_rmsnorm_kernel, segments=segments),
        grid=(n_tiles,),
        in_specs=[_segment_spec((ROW_TILE, d), seg, t0, 0, lambda i: 0)
                  for seg, (_, t0) in zip(segments, src)]
        + [pl.BlockSpec((1, d), lambda i: (0, 0))],
        out_specs=pl.BlockSpec((ROW_TILE, d), lambda i: (i, 0)),
        out_shape=jax.ShapeDtypeStruct((n_tiles * ROW_TILE, d), out_dtype),
        compiler_params=_params(("arbitrary",)),
        name="rmsnorm",
    )(*[p for p, _ in src], w.reshape(1, d))


def _rmsnorm_rows(x, w, out_dtype, row0, n_rows):
    d = x.shape[1]
    assert row0 % ROW_TILE == 0 and n_rows % ROW_TILE == 0
    seg = (0, n_rows // ROW_TILE)
    return pl.pallas_call(
        functools.partial(_rmsnorm_kernel, segments=[seg]),
        grid=(n_rows // ROW_TILE,),
        in_specs=[_segment_spec((ROW_TILE, d), seg, row0 // ROW_TILE, 0, lambda i: 0),
                  pl.BlockSpec((1, d), lambda i: (0, 0))],
        out_specs=pl.BlockSpec((ROW_TILE, d), lambda i: (i, 0)),
        out_shape=jax.ShapeDtypeStruct((n_rows, d), out_dtype),
        compiler_params=_params(("arbitrary",)),
        name="rmsnorm",
    )(x, w.reshape(1, d))


def _matmul_kernel(*refs, segments, has_res, w_is_nk):
    n = len(segments)
    a_refs, w_ref = refs[:n], refs[n]
    r_refs = refs[n + 1:2 * n + 1] if has_res else None
    o_ref, wb_ref = refs[-2:]
    i = pl.program_id(1)

    @pl.when(i == 0)
    def _():
        wb_ref[...] = w_ref[...].astype(BF16)

    def run(k):
        acc = (_dot_nt if w_is_nk else _dot)(a_refs[k][...], wb_ref[...])
        o_ref[...] = acc + r_refs[k][...] if has_res else acc

    _for_current_segment(segments, i, run)


def _matmul(a, w, n_cols, tm, tn, res=None, layer=None, w_is_nk=False):
    segments, sources = _row_segments([a] if res is None else [a, res], tm)
    k = sources[0][0][0].shape[1]
    m_tiles = segments[-1][0] + segments[-1][1]
    in_specs = [_segment_spec((tm, k), seg, t0, 1, lambda j, i: 0)
                for seg, (_, t0) in zip(segments, sources[0])]
    if w_is_nk:
        in_specs.append(pl.BlockSpec((tn, k), lambda j, i: (j, 0)))
    elif layer is None:
        in_specs.append(pl.BlockSpec((k, tn), lambda j, i: (0, j)))
    else:
        in_specs.append(pl.BlockSpec((None, k, tn), lambda j, i: (layer, 0, j)))
    args = [p for p, _ in sources[0]] + [w]
    if res is not None:
        in_specs += [_segment_spec((tm, tn), seg, t0, 1, lambda j, i: j)
                     for seg, (_, t0) in zip(segments, sources[1])]
        args += [p for p, _ in sources[1]]
    return pl.pallas_call(
        functools.partial(_matmul_kernel, segments=segments, has_res=res is not None,
                          w_is_nk=w_is_nk),
        grid=(n_cols // tn, m_tiles),
        in_specs=in_specs,
        out_specs=pl.BlockSpec((tm, tn), lambda j, i: (i, j)),
        out_shape=jax.ShapeDtypeStruct((m_tiles * tm, n_cols), F32),
        scratch_shapes=[pltpu.VMEM((tn, k) if w_is_nk else (k, tn), BF16)],
        compiler_params=_params(("arbitrary", "arbitrary")),
        name="matmul",
    )(*args)


def _ret_kernel(lg_ref, q_ref, k_ref, v_ref, g_ref, cos_ref, sin_ref, gn_ref, *rest,
                chunk, n_seq, chunks_per_seq, has_s0):
    if has_s0:
        s0a_ref, s0b_ref, o_ref, s_ref = rest
    else:
        o_ref, s_ref = rest
    c = chunk
    h = pl.program_id(1)
    lg = lg_ref[h]

    if not has_s0:
        @pl.when(pl.program_id(2) == 0)
        def _():
            s_ref[...] = jnp.zeros(s_ref.shape, F32)

    ii = lax.broadcasted_iota(jnp.int32, (c, 1), 0).astype(F32)
    jj = lax.broadcasted_iota(jnp.int32, (1, c), 1).astype(F32)
    diff = ii - jj
    causal = diff >= 0.0
    decay = jnp.where(causal, jnp.exp(lg * jnp.where(causal, diff, 0.0)), 0.0)
    q_decay = jnp.exp(lg * (ii + 1.0))
    k_decay = jnp.exp(lg * (c - 1.0 - ii))
    chunk_decay = jnp.exp(jnp.full((1, 1), lg * c, F32))
    half = RET_DK // 2
    k_scale = RET_DK ** -0.5
    gn_w = gn_ref[...]
    op = (lambda t: t.astype(BF16)) if c % BF16_ROWS == 0 else (lambda t: t)
    outs = []

    for s in range(n_seq):
        state = jnp.concatenate([s0a_ref[s], s0b_ref[s]], axis=0) if has_s0 else s_ref[s]
        for ci in range(chunks_per_seq):
            r0 = (s * chunks_per_seq + ci) * c
            rows = slice(r0, r0 + c)
            cos = cos_ref[rows, :]
            sin = sin_ref[rows, :]
            q = q_ref[rows, :]
            k = k_ref[rows, :]
            q1, q2 = q[:, :half], q[:, half:]
            k1, k2 = k[:, :half], k[:, half:]
            qr = jnp.concatenate([q1 * cos - q2 * sin, q1 * sin + q2 * cos], axis=-1)
            kr = jnp.concatenate([k1 * cos - k2 * sin, k1 * sin + k2 * cos], axis=-1) * k_scale
            vb = op(v_ref[rows, :])
            qb = op(qr)
            scores = _dot_nt(qb, op(kr)) * decay
            inner = _dot(op(scores), vb)
            cross = _dot(qb, op(state)) * q_decay
            o = inner + cross
            state = state * chunk_decay + _dot_tn(op(kr * k_decay), vb)
            mu = jnp.mean(o, axis=-1, keepdims=True)
            oc = o - mu
            var = jnp.mean(oc * oc, axis=-1, keepdims=True)
            on = oc * lax.rsqrt(var + EPS) * gn_w
            res = on * _silu(g_ref[rows, :])
            if c % BF16_ROWS == 0:
                o_ref[rows, :] = res.astype(o_ref.dtype)
            else:
                outs.append(res)
        s_ref[s] = state
    if outs:
        o_ref[...] = jnp.concatenate(outs, axis=0).astype(o_ref.dtype)


def _retention(proj, row0, n_seq_total, seq_len, pos0, s0, gn_w, *, chunk, tb, bb):
    hq = RET_HEADS * RET_DK
    n_rows = n_seq_total * seq_len
    blk_rows = bb * tb
    n_inner = seq_len // tb
    n_outer = n_seq_total // bb
    assert row0 % blk_rows == 0 and seq_len % tb == 0 and tb % chunk == 0
    assert s0 is None or n_inner == 1
    rb0 = row0 // blk_rows
    half = RET_DK // 2

    pos = (pos0 + jnp.arange(seq_len)).astype(F32)
    inv = ROPE_BASE ** (-jnp.arange(half, dtype=F32) / half)
    ang = pos[:, None] * inv[None, :]
    cos = jnp.tile(jnp.cos(ang), (bb, 1)) if n_inner == 1 else jnp.cos(ang)
    sin = jnp.tile(jnp.sin(ang), (bb, 1)) if n_inner == 1 else jnp.sin(ang)
    log_gamma = jnp.log1p(-jnp.exp2(-5.0 - jnp.arange(RET_HEADS, dtype=F32)))

    def row_map(b, h, n):
        return rb0 + b * n_inner + n

    in_specs = [
        pl.BlockSpec(memory_space=pltpu.SMEM),
        pl.BlockSpec((blk_rows, RET_DK), lambda b, h, n: (row_map(b, h, n), h)),
        pl.BlockSpec((blk_rows, RET_DK), lambda b, h, n: (row_map(b, h, n), RET_HEADS + h)),
        pl.BlockSpec((blk_rows, RET_DV), lambda b, h, n: (row_map(b, h, n), 2 * hq // RET_DV + h)),
        pl.BlockSpec((blk_rows, RET_DV), lambda b, h, n: (row_map(b, h, n), 2 * hq // RET_DV + RET_HEADS + h)),
        pl.BlockSpec((blk_rows, half), lambda b, h, n: (n, 0)),
        pl.BlockSpec((blk_rows, half), lambda b, h, n: (n, 0)),
        pl.BlockSpec((None, 1, RET_DV), lambda b, h, n: (h, 0, 0)),
    ]
    args = [log_gamma, proj, proj, proj, proj, cos, sin, gn_w.reshape(RET_HEADS, 1, RET_DV)]
    if s0 is not None:
        in_specs += [pl.BlockSpec((bb, None, RET_DK // 2, RET_DV), lambda b, h, n: (b, h, 0, 0)),
                     pl.BlockSpec((bb, None, RET_DK // 2, RET_DV), lambda b, h, n: (b, h, 1, 0))]
        args += [s0, s0]
    out, s_fin = pl.pallas_call(
        functools.partial(_ret_kernel, chunk=chunk, n_seq=bb, chunks_per_seq=tb // chunk,
                          has_s0=s0 is not None),
        grid=(n_outer, RET_HEADS, n_inner),
        in_specs=in_specs,
        out_specs=[pl.BlockSpec((blk_rows, RET_DV), lambda b, h, n: (b * n_inner + n, h)),
                   pl.BlockSpec((bb, None, RET_DK, RET_DV), lambda b, h, n: (b, h, 0, 0))],
        out_shape=[jax.ShapeDtypeStruct((n_rows, RET_HEADS * RET_DV), BF16),
                   jax.ShapeDtypeStruct((n_seq_total, RET_HEADS, RET_DK, RET_DV), F32)],
        compiler_params=_params(("parallel", "parallel", "arbitrary")),
        name="retention",
    )(*args)
    return out, s_fin


def _shift_rows(x, prev8, s, x_ref=None):
    n = x.shape[0]
    first = x[:SUBLANES]
    row = lax.broadcasted_iota(jnp.int32, first.shape, 0)
    top = jnp.where(row < s, pltpu.roll(prev8, s, 0), pltpu.roll(first, s, 0))
    if n == SUBLANES:
        return top
    if x_ref is None:
        rest = pltpu.roll(x, s, 0)[SUBLANES:]
    else:
        rest = x_ref[pl.ds(SUBLANES - s, n - SUBLANES), :]
    return jnp.concatenate([top, rest], axis=0)


def _causal_conv_silu(x, prev8, cw, x_ref=None):
    acc = x * cw[CONV_W - 1:CONV_W, :]
    for s in range(1, CONV_W):
        acc = acc + _shift_rows(x, prev8, s, x_ref) * cw[CONV_W - 1 - s:CONV_W - s, :]
    return _silu(acc)


def _gdn_kernel(q_ref, k_ref, v_ref, z_ref, gb_ref, alog_ref, dtb_ref, cq_ref, ck_ref, cv_ref,
                nw_ref, *rest, chunk, group, n_seq, rows_per_seq, carry):
    has_init = not carry
    if has_init:
        pq_ref, pk_ref, pv_ref, s0_ref, o_ref, s_ref, prev_ref = rest
    else:
        o_ref, s_ref, prev_ref = rest
    c = chunk
    r = n_seq * rows_per_seq
    m = group
    dk = GDN_DK
    op = (lambda t: t.astype(BF16)) if c % BF16_ROWS == 0 else (lambda t: t)

    if carry:
        @pl.when(pl.program_id(2) == 0)
        def _():
            s_ref[...] = jnp.zeros(s_ref.shape, F32)
            prev_ref[...] = jnp.zeros(prev_ref.shape, F32)

    q_raw, k_raw, v_raw = q_ref[...], k_ref[...], v_ref[...]
    if carry:
        prev = prev_ref[...]
        qc = _causal_conv_silu(q_raw, prev[:, :dk], cq_ref[...], q_ref)
        kc = _causal_conv_silu(k_raw, prev[:, dk:2 * dk], ck_ref[...], k_ref)
        vc = _causal_conv_silu(v_raw, prev[:, 2 * dk:], cv_ref[...], v_ref)
        prev_ref[...] = jnp.concatenate(
            [q_raw[r - SUBLANES:], k_raw[r - SUBLANES:], v_raw[r - SUBLANES:]], axis=-1)
    else:
        qs, ks, vs = [], [], []
        for s in range(n_seq):
            rows = slice(s * rows_per_seq, (s + 1) * rows_per_seq)
            qs.append(_causal_conv_silu(q_raw[rows], pq_ref[s], cq_ref[...]))
            ks.append(_causal_conv_silu(k_raw[rows], pk_ref[s], ck_ref[...]))
            vs.append(_causal_conv_silu(v_raw[rows], pv_ref[s], cv_ref[...]))
        qc = jnp.concatenate(qs, axis=0)
        kc = jnp.concatenate(ks, axis=0)
        vc = jnp.concatenate(vs, axis=0)

    qn = qc * lax.rsqrt(jnp.sum(qc * qc, axis=-1, keepdims=True) + EPS) * (dk ** -0.5)
    kn = kc * lax.rsqrt(jnp.sum(kc * kc, axis=-1, keepdims=True) + EPS)
    qb = qn.astype(BF16)
    kb = kn.astype(BF16)

    gb = gb_ref[...]
    a_coef = jnp.exp(alog_ref[...][:, :1])
    xs = gb + dtb_ref[...][:, :1]
    softplus = jnp.maximum(xs, 0.0) + jnp.log(1.0 + jnp.exp(-jnp.abs(xs)))
    g_rows = -a_coef * softplus
    beta_rows = 1.0 / (1.0 + jnp.exp(-gb))
    rid = lax.broadcasted_iota(jnp.int32, gb.shape, 0)
    x8 = jnp.where(rid < 2, g_rows, beta_rows)

    ri = lax.broadcasted_iota(jnp.int32, (m, m), 0)
    ci = lax.broadcasted_iota(jnp.int32, (m, m), 1)
    log2c = int(math.log2(c))
    same = lax.shift_right_logical(ri, log2c) == lax.shift_right_logical(ci, log2c)
    tri_incl = jnp.logical_and(same, ri >= ci)
    tri_strict = jnp.logical_and(same, ri > ci)
    tri_b = jnp.where(tri_incl, 1.0, 0.0).astype(BF16)
    eye_b = jnp.where(ri == ci, 1.0, 0.0).astype(BF16)
    eye_f = jnp.where(ri == ci, 1.0, 0.0)
    n_sq = max(log2c - 1, 0)
    nw = nw_ref[...]

    units = [(g, vh) for g in range(r // m) for vh in range(2)]
    rows_of = {g: slice(g * m, (g + 1) * m) for g in range(r // m)}
    gate = {}
    for g in range(r // m):
        rows_g = rows_of[g]
        parts = _split3(x8[:, rows_g])
        cs_col = sum(_dot_nt(tri_b, p) for p in parts)
        cs_row = sum(_dot_nt(p, tri_b) for p in parts)
        x_col = sum(_dot_nt(eye_b, p) for p in parts)
        gate[g] = (cs_col, cs_row, x_col, jnp.exp(cs_col),
                   _dot_nt(kb[rows_g], kb[rows_g]), _dot_nt(qb[rows_g], kb[rows_g]))

    g_col, b_col, eg_col, dmat, t_mat, pw = {}, {}, {}, {}, {}, {}
    for u in units:
        g, vh = u
        cs_col, cs_row, x_col, exp_col, kk, _ = gate[g]
        g_col[u] = cs_col[:, vh:vh + 1]
        b_col[u] = x_col[:, 2 + vh:3 + vh]
        eg_col[u] = exp_col[:, vh:vh + 1]
        dmat[u] = jnp.exp(jnp.where(tri_incl, g_col[u] - cs_row[vh:vh + 1, :], -jnp.inf))
        pw[u] = jnp.where(tri_strict, -(kk * dmat[u]) * b_col[u], 0.0)
        t_mat[u] = eye_f + pw[u]
    for _ in range(n_sq):
        for u in units:
            pwb = pw[u].astype(BF16)
            pw[u] = _dot(pwb, pwb)
        for u in units:
            t_mat[u] = t_mat[u] + _dot(t_mat[u].astype(BF16), pw[u].astype(BF16))

    sol, q_in = {}, {}
    for u in units:
        g, vh = u
        v_h = vc[rows_of[g], vh * GDN_DV:(vh + 1) * GDN_DV]
        rhs = jnp.concatenate([v_h * b_col[u], kn[rows_of[g]] * (b_col[u] * eg_col[u])], axis=-1)
        sol[u] = _dot(t_mat[u].astype(BF16), rhs.astype(BF16))
        q_in[u] = qn[rows_of[g]] * eg_col[u]

    chunks = [(u, ch) for u in units for ch in range(m // c)]
    k_dec, g_last, qp = {}, {}, {}
    for uc in chunks:
        u, ch = uc
        rows = slice(ch * c, (ch + 1) * c)
        g_last[uc] = g_col[u][(ch + 1) * c - 1:(ch + 1) * c, :]
        k_dec[uc] = kn[rows_of[u[0]]][rows] * jnp.exp(g_last[uc] - g_col[u][rows])
        if carry:
            qp[uc] = _dot_tn(k_dec[uc].astype(BF16), sol[u][rows].astype(BF16))

    v_new, o_inter = {}, {}
    if carry:
        states = [s_ref[0, vh] for vh in range(2)]
        sb = {}
        for g in range(r // m):
            for ch in range(m // c):
                for vh in range(2):
                    uc = ((g, vh), ch)
                    sb[uc] = states[vh].astype(BF16)
                    states[vh] = (states[vh] * jnp.exp(g_last[uc])
                                  - _dot(qp[uc][:, GDN_DV:].astype(BF16), sb[uc]) + qp[uc][:, :GDN_DV])
        for vh in range(2):
            s_ref[0, vh] = states[vh]
        for uc in chunks:
            u, ch = uc
            rows = slice(ch * c, (ch + 1) * c)
            both = _dot(jnp.concatenate([sol[u][rows, GDN_DV:].astype(BF16),
                                         q_in[u][rows].astype(BF16)], axis=0), sb[uc])
            v_new[uc] = sol[u][rows, :GDN_DV] - both[:c]
            o_inter[uc] = both[c:]
    else:
        for uc in chunks:
            u, ch = uc
            rows = slice(ch * c, (ch + 1) * c)
            seq = (u[0] * m + ch * c) // rows_per_seq
            state = s0_ref[seq, u[1]]
            both = _dot(jnp.concatenate([sol[u][rows, GDN_DV:], q_in[u][rows]], axis=0).astype(BF16),
                        state.astype(BF16))
            v_new[uc] = sol[u][rows, :GDN_DV] - both[:c]
            o_inter[uc] = both[c:]
            s_ref[seq, u[1]] = (state * jnp.exp(g_last[uc])
                                + _dot_tn(op(k_dec[uc]), op(v_new[uc])))

    for u in units:
        g, vh = u
        v_new_all = jnp.concatenate([v_new[(u, ch)] for ch in range(m // c)], axis=0)
        o = jnp.concatenate([o_inter[(u, ch)] for ch in range(m // c)], axis=0)
        attn = gate[g][5] * dmat[u]
        o = o + _dot(attn.astype(BF16), v_new_all.astype(BF16))
        o = o * lax.rsqrt(jnp.mean(o * o, axis=-1, keepdims=True) + EPS) * nw
        z = z_ref[rows_of[g], vh * GDN_DV:(vh + 1) * GDN_DV]
        o_ref[rows_of[g], vh * GDN_DV:(vh + 1) * GDN_DV] = (o * _silu(z)).astype(o_ref.dtype)


def _gdn(proj, gates, row0, n_seq_total, seq_len, conv0_pad, s0, conv_w, a_log, dt_bias, norm_w,
         *, chunk, tb, bb):
    hk = GDN_K_HEADS * GDN_DK
    hv = GDN_V_HEADS * GDN_DV
    n_rows = n_seq_total * seq_len
    blk_rows = bb * tb
    n_inner = seq_len // tb
    n_outer = n_seq_total // bb
    carry = s0 is None
    assert row0 % blk_rows == 0 and seq_len % tb == 0 and tb % chunk == 0
    assert carry == (bb == 1) and (carry or (n_inner == 1 and tb == chunk))
    assert blk_rows % GDN_GROUP == 0 and GDN_GROUP % chunk == 0
    rb0 = row0 // blk_rows
    two = 2 * GDN_DV

    def row_map(b, h, n):
        return rb0 + b * n_inner + n

    alog8 = jnp.zeros((GDN_K_HEADS, SUBLANES, LANES), F32).at[:, :2, :].set(
        jnp.broadcast_to(a_log.reshape(GDN_K_HEADS, 2, 1), (GDN_K_HEADS, 2, LANES)))
    dtb8 = jnp.zeros((GDN_K_HEADS, SUBLANES, LANES), F32).at[:, :2, :].set(
        jnp.broadcast_to(dt_bias.reshape(GDN_K_HEADS, 2, 1), (GDN_K_HEADS, 2, LANES)))

    in_specs = [
        pl.BlockSpec((blk_rows, GDN_DK), lambda b, h, n: (row_map(b, h, n), h)),
        pl.BlockSpec((blk_rows, GDN_DK), lambda b, h, n: (row_map(b, h, n), GDN_K_HEADS + h)),
        pl.BlockSpec((blk_rows, two), lambda b, h, n: (row_map(b, h, n), 2 * hk // two + h)),
        pl.BlockSpec((blk_rows, two), lambda b, h, n: (row_map(b, h, n), CONV_DIM // two + h)),
        pl.BlockSpec((None, SUBLANES, blk_rows), lambda b, h, n: (h, 0, row_map(b, h, n))),
        pl.BlockSpec((None, SUBLANES, LANES), lambda b, h, n: (h, 0, 0)),
        pl.BlockSpec((None, SUBLANES, LANES), lambda b, h, n: (h, 0, 0)),
        pl.BlockSpec((CONV_W, GDN_DK), lambda b, h, n: (0, h)),
        pl.BlockSpec((CONV_W, GDN_DK), lambda b, h, n: (0, GDN_K_HEADS + h)),
        pl.BlockSpec((CONV_W, two), lambda b, h, n: (0, 2 * hk // two + h)),
        pl.BlockSpec((1, GDN_DV), lambda b, h, n: (0, 0)),
    ]
    args = [proj, proj, proj, proj, gates, alog8, dtb8, conv_w, conv_w, conv_w,
            norm_w.reshape(1, GDN_DV)]
    if not carry:
        in_specs += [
            pl.BlockSpec((bb, SUBLANES, GDN_DK), lambda b, h, n: (b, 0, h)),
            pl.BlockSpec((bb, SUBLANES, GDN_DK), lambda b, h, n: (b, 0, GDN_K_HEADS + h)),
            pl.BlockSpec((bb, SUBLANES, two), lambda b, h, n: (b, 0, 2 * hk // two + h)),
            pl.BlockSpec((bb, 2, GDN_DK, GDN_DV), lambda b, h, n: (b, h, 0, 0)),
        ]
        args += [conv0_pad, conv0_pad, conv0_pad, s0]
    out, s_fin = pl.pallas_call(
        functools.partial(_gdn_kernel, chunk=chunk, group=GDN_GROUP, n_seq=bb, rows_per_seq=tb,
                          carry=carry),
        grid=(n_outer, GDN_K_HEADS, n_inner),
        in_specs=in_specs,
        out_specs=[pl.BlockSpec((blk_rows, two), lambda b, h, n: (b * n_inner + n, h)),
                   pl.BlockSpec((bb, 2, GDN_DK, GDN_DV), lambda b, h, n: (b, h, 0, 0))],
        out_shape=[jax.ShapeDtypeStruct((n_rows, hv), BF16),
                   jax.ShapeDtypeStruct((n_seq_total, GDN_V_HEADS, GDN_DK, GDN_DV), F32)],
        scratch_shapes=[pltpu.VMEM((SUBLANES, 2 * GDN_DK + two), F32)],
        compiler_params=_params(("parallel", "parallel", "arbitrary")),
        name="gated_delta",
    )(*args)
    return out, s_fin


def _router_kernel(x_ref, w_ref, wr_ref, br_ref, h_ref, lg_ref):
    x = x_ref[...]
    ms = jnp.mean(x * x, axis=-1, keepdims=True)
    h = x * lax.rsqrt(ms + EPS) * w_ref[...]
    h_ref[...] = h
    lg_ref[...] = _dot_nt(wr_ref[...].astype(BF16), h.astype(BF16)) + br_ref[...]


def _router(x, norm_w, w_route, b_route):
    t, d = x.shape
    return pl.pallas_call(
        _router_kernel,
        grid=(t // ROW_TILE,),
        in_specs=[pl.BlockSpec((ROW_TILE, d), lambda i: (i, 0)),
                  pl.BlockSpec((1, d), lambda i: (0, 0)),
                  pl.BlockSpec((LANES, d), lambda i: (0, 0)),
                  pl.BlockSpec((LANES, 1), lambda i: (0, 0))],
        out_specs=[pl.BlockSpec((ROW_TILE, d), lambda i: (i, 0)),
                   pl.BlockSpec((LANES, ROW_TILE), lambda i: (0, i))],
        out_shape=[jax.ShapeDtypeStruct((t, d), F32),
                   jax.ShapeDtypeStruct((LANES, t), F32)],
        compiler_params=_params(("parallel",)),
        name="moe_router",
    )(x, norm_w.reshape(1, d), w_route, b_route)


def _row_gather_start(idx_ref, base, n, src_hbm, dst_ref, sem):
    def body(r, carry):
        tok = idx_ref[base + r]
        pltpu.make_async_copy(src_hbm.at[pl.ds(tok, 1)], dst_ref.at[pl.ds(r, 1)], sem).start()
        return carry
    lax.fori_loop(0, n, body, 0, unroll=GATHER_UNROLL)


def _row_gather_wait(n, src_hbm, dst_ref, sem):
    pltpu.make_async_copy(src_hbm.at[pl.ds(0, n)], dst_ref, sem).wait()


def _row_gather_start_counted(idx_ref, base, n, src_hbm, dst_ref, sem):
    def body(g, carry):
        for k in range(GATHER_UNROLL):
            r = g * GATHER_UNROLL + k
            tok = idx_ref[base + r]
            pltpu.make_async_copy(src_hbm.at[pl.ds(tok, 1)], dst_ref.at[pl.ds(r, 1)], sem).start()
        return carry
    lax.fori_loop(0, n // GATHER_UNROLL, body, 0)


def _row_gather_wait_counted(n, src_hbm, dst_ref, sem):
    @pl.when(n > 0)
    def _():
        rows = pl.multiple_of(n, GATHER_UNROLL)
        pltpu.make_async_copy(src_hbm.at[pl.ds(0, rows)], dst_ref.at[pl.ds(0, rows)], sem).wait()


def _moe_ffn_kernel(be_ref, first_ref, nxt_ref, ws_ref, cnt_ref, rt_ref, na_ref, h_hbm, rw_ref,
                    win_hbm, wout_hbm, o_ref, xbuf, sem, win_f, wout_f, win_sem, wout_sem, winb, woutb,
                    *, layer):
    i = pl.program_id(0)
    n_active = na_ref[0]
    slot = lax.rem(i, MOE_GATHER_SLOTS)

    def gather_start(blk):
        s = lax.rem(blk, MOE_GATHER_SLOTS)
        _row_gather_start_counted(rt_ref, blk * MOE_BLOCK, cnt_ref[blk], h_hbm, xbuf.at[s], sem.at[s])

    @pl.when(i == 0)
    def _():
        xbuf[...] = jnp.zeros(xbuf.shape, F32)

    def weight_copies(e, ws):
        r_in = win_f.shape[1] // MOE_W_SLABS
        r_out = wout_f.shape[1] // MOE_W_SLABS
        cps = []
        for k in range(MOE_W_SLABS):
            cps.append(pltpu.make_async_copy(win_hbm.at[layer, e, pl.ds(k * r_in, r_in)],
                                             win_f.at[ws, pl.ds(k * r_in, r_in)], win_sem.at[ws, k]))
            cps.append(pltpu.make_async_copy(wout_hbm.at[layer, e, pl.ds(k * r_out, r_out)],
                                             wout_f.at[ws, pl.ds(k * r_out, r_out)], wout_sem.at[ws, k]))
        return cps

    @pl.when(jnp.logical_and(i == 0, n_active > 0))
    def _():
        for cp in weight_copies(be_ref[0], ws_ref[0]):
            cp.start(priority=BULK_DMA_PRIORITY)
        for blk in range(MOE_GATHER_SLOTS - 1):
            pl.when(blk < n_active)(functools.partial(gather_start, blk))

    @pl.when(i + MOE_GATHER_SLOTS - 1 < n_active)
    def _():
        gather_start(i + MOE_GATHER_SLOTS - 1)

    @pl.when(i < n_active)
    def _():
        ws = ws_ref[i]

        @pl.when(first_ref[i] == 1)
        def _():
            @pl.when(nxt_ref[i] >= 0)
            def _():
                for cp in weight_copies(nxt_ref[i], 1 - ws):
                    cp.start(priority=BULK_DMA_PRIORITY)

            for cp in weight_copies(be_ref[i], ws):
                cp.wait()
            winb[...] = win_f[ws].astype(BF16)
            woutb[...] = wout_f[ws].astype(BF16)

        _row_gather_wait_counted(cnt_ref[i], h_hbm, xbuf.at[slot], sem.at[slot])
        x = xbuf[slot].astype(BF16)
        mid = _dot(x, winb[...])
        act = (_silu(mid[:, :D_EXPERT]) * mid[:, D_EXPERT:]).astype(BF16)
        o_ref[...] = _dot(act, woutb[...]) * rw_ref[...]

    @pl.when(i >= n_active)
    def _():
        o_ref[...] = jnp.zeros(o_ref.shape, F32)


def _moe_ffn(h, block_e, block_first, block_next, block_wslot, block_cnt, row_tok, n_active, row_w,
             w_in, w_out, layer):
    t, d = h.shape
    n_blocks = block_e.shape[0]
    n_rows = n_blocks * MOE_BLOCK
    grid_spec = pltpu.PrefetchScalarGridSpec(
        num_scalar_prefetch=7,
        grid=(n_blocks,),
        in_specs=[
            pl.BlockSpec(memory_space=pl.ANY),
            pl.BlockSpec((MOE_BLOCK, 1), lambda i, *_: (i, 0)),
            pl.BlockSpec(memory_space=pl.ANY),
            pl.BlockSpec(memory_space=pl.ANY),
        ],
        out_specs=pl.BlockSpec((MOE_BLOCK, d), lambda i, *_: (i, 0)),
        scratch_shapes=[pltpu.VMEM((MOE_GATHER_SLOTS, MOE_BLOCK, d), F32),
                        pltpu.SemaphoreType.DMA((MOE_GATHER_SLOTS,)),
                        pltpu.VMEM((2, d, 2 * D_EXPERT), F32),
                        pltpu.VMEM((2, D_EXPERT, d), F32),
                        pltpu.SemaphoreType.DMA((2, MOE_W_SLABS)),
                        pltpu.SemaphoreType.DMA((2, MOE_W_SLABS)),
                        pltpu.VMEM((d, 2 * D_EXPERT), BF16),
                        pltpu.VMEM((D_EXPERT, d), BF16)],
    )
    return pl.pallas_call(
        functools.partial(_moe_ffn_kernel, layer=layer),
        grid_spec=grid_spec,
        out_shape=jax.ShapeDtypeStruct((n_rows, d), F32),
        compiler_params=_params(("arbitrary",)),
        name="moe_ffn",
    )(block_e, block_first, block_next, block_wslot, block_cnt, row_tok, n_active, h,
      row_w.reshape(n_rows, 1), w_in, w_out)


def _combine_kernel(pos_ref, ys_hbm, x_ref, *rest, with_norm):
    if with_norm:
        nw_ref, o_ref, hn_ref, buf, sem = rest
    else:
        o_ref, buf, sem = rest
    i = pl.program_id(0)
    n = pl.num_programs(0)
    slot = lax.rem(i, COMBINE_SLOTS)
    rows = 2 * COMBINE_TM

    def gather_start(tile):
        s = lax.rem(tile, COMBINE_SLOTS)
        _row_gather_start(pos_ref, tile * rows, rows, ys_hbm, buf.at[s], sem.at[s])

    @pl.when(i == 0)
    def _():
        for tile in range(COMBINE_SLOTS - 1):
            pl.when(tile < n)(functools.partial(gather_start, tile))

    @pl.when(i + COMBINE_SLOTS - 1 < n)
    def _():
        gather_start(i + COMBINE_SLOTS - 1)

    _row_gather_wait(rows, ys_hbm, buf.at[slot], sem.at[slot])
    y = x_ref[...] + buf[slot, :COMBINE_TM, :] + buf[slot, COMBINE_TM:, :]
    o_ref[...] = y
    if with_norm:
        ms = jnp.mean(y * y, axis=-1, keepdims=True)
        hn_ref[...] = (y * lax.rsqrt(ms + EPS) * nw_ref[...]).astype(hn_ref.dtype)


def _combine(x, ys, pos, next_norm_w=None):
    t, d = x.shape
    with_norm = next_norm_w is not None
    row_spec = pl.BlockSpec((COMBINE_TM, d), lambda i, p: (i, 0))
    in_specs = [pl.BlockSpec(memory_space=pl.ANY), row_spec]
    args = [pos, ys, x]
    out_specs, out_shape = row_spec, jax.ShapeDtypeStruct((t, d), F32)
    if with_norm:
        in_specs.append(pl.BlockSpec((1, d), lambda i, p: (0, 0)))
        args.append(next_norm_w.reshape(1, d))
        out_specs = [row_spec, row_spec]
        out_shape = [out_shape, jax.ShapeDtypeStruct((t, d), BF16)]
    grid_spec = pltpu.PrefetchScalarGridSpec(
        num_scalar_prefetch=1,
        grid=(t // COMBINE_TM,),
        in_specs=in_specs,
        out_specs=out_specs,
        scratch_shapes=[pltpu.VMEM((COMBINE_SLOTS, 2 * COMBINE_TM, d), F32),
                        pltpu.SemaphoreType.DMA((COMBINE_SLOTS,))],
    )
    return pl.pallas_call(
        functools.partial(_combine_kernel, with_norm=with_norm),
        grid_spec=grid_spec,
        out_shape=out_shape,
        compiler_params=_params(("arbitrary",)),
        name="moe_combine",
    )(*args)


def _hier_moe(x, norm_w, w_group, b_group, w_expert, b_expert, w_in, w_out, layer, next_norm_w=None):
    t, d = x.shape
    n_route = MOE_GROUPS + N_EXPERTS
    w_route = jnp.zeros((LANES, d), F32).at[:MOE_GROUPS].set(w_group.T).at[MOE_GROUPS:n_route].set(w_expert.T)
    b_route = jnp.zeros((LANES, 1), F32).at[:MOE_GROUPS, 0].set(b_group).at[MOE_GROUPS:n_route, 0].set(b_expert)
    h, logits = _router(x, norm_w, w_route, b_route)

    g_logits = logits[:MOE_GROUPS]
    g_max = jnp.max(g_logits, axis=0)
    g_sel = jnp.argmax(g_logits, axis=0).astype(jnp.int32)
    g_w = 1.0 / jnp.sum(jnp.exp(g_logits - g_max[None, :]), axis=0)
    e_all = logits[MOE_GROUPS:n_route].reshape(MOE_GROUPS, EXPERTS_PER_GROUP, t)
    grp = jnp.arange(MOE_GROUPS, dtype=jnp.int32)[:, None, None]
    e_logits = jnp.sum(jnp.where(grp == g_sel[None, None, :], e_all, 0.0), axis=0)
    slot_id = jnp.arange(EXPERTS_PER_GROUP, dtype=jnp.int32)[:, None]
    i1 = jnp.argmax(e_logits, axis=0).astype(jnp.int32)
    v1 = jnp.max(e_logits, axis=0)
    rest = jnp.where(slot_id == i1[None, :], -jnp.inf, e_logits)
    i2 = jnp.argmax(rest, axis=0).astype(jnp.int32)
    v2 = jnp.max(rest, axis=0)
    e21 = jnp.exp(v2 - v1)
    weights = jnp.stack([g_w / (1.0 + e21), g_w * e21 / (1.0 + e21)], axis=1)
    expert_id = jnp.stack([g_sel * EXPERTS_PER_GROUP + i1, g_sel * EXPERTS_PER_GROUP + i2], axis=1)

    n_assign = t * TOP_K
    flat_e = expert_id.reshape(n_assign)
    iota = jnp.arange(n_assign, dtype=jnp.int32)
    experts = jnp.arange(N_EXPERTS, dtype=jnp.int32)
    sorted_e, order, w_sorted = lax.sort((flat_e, iota, weights.reshape(n_assign)), num_keys=1,
                                         is_stable=True)
    counts = jnp.sum((flat_e[:, None] == experts[None, :]).astype(jnp.int32), axis=0)
    start = jnp.cumsum(counts) - counts
    nblk_e = (counts + MOE_BLOCK - 1) // MOE_BLOCK
    blk_end = jnp.cumsum(nblk_e)
    blk_start = blk_end - nblk_e

    def lookup(idx, table):
        return jnp.sum(jnp.where(idx[:, None] == experts[None, :], table[None, :], 0), axis=1)

    dest_sorted = iota + lookup(sorted_e, blk_start * MOE_BLOCK - start)
    _, dest = lax.sort((order, dest_sorted), num_keys=1)

    n_blocks = -(-n_assign // MOE_BLOCK) + N_EXPERTS
    n_active = blk_end[-1].astype(jnp.int32)
    blk = jnp.minimum(jnp.arange(n_blocks, dtype=jnp.int32), jnp.maximum(n_active - 1, 0))
    block_e = jnp.minimum(jnp.sum((blk[:, None] >= blk_end[None, :]).astype(jnp.int32), axis=1),
                          N_EXPERTS - 1)
    block_first = (blk == lookup(block_e, blk_start)).astype(jnp.int32)
    active = counts > 0
    ordinal = jnp.cumsum(active.astype(jnp.int32)) - 1
    block_wslot = lookup(block_e, ordinal) % 2
    later = lax.cummin(jnp.where(active, experts, N_EXPERTS), reverse=True)
    next_e = jnp.concatenate([later[1:], jnp.full((1,), N_EXPERTS, jnp.int32)])
    block_next = lookup(block_e, jnp.where(next_e >= N_EXPERTS, -1, next_e))

    j_base = lookup(block_e, start) + (blk - lookup(block_e, blk_start)) * MOE_BLOCK
    j_end = lookup(block_e, start + counts)
    j = j_base[:, None] + jnp.arange(MOE_BLOCK, dtype=jnp.int32)[None, :]
    valid = jnp.logical_and(j < j_end[:, None],
                            jnp.arange(n_blocks, dtype=jnp.int32)[:, None] < n_active)
    j = jnp.clip(j, 0, n_assign - 1).reshape(-1)
    valid = valid.reshape(-1)
    row_tok = jnp.where(valid, order[j] // TOP_K, 0)
    row_w = jnp.where(valid, w_sorted[j], 0.0)
    block_cnt = jnp.where(jnp.arange(n_blocks, dtype=jnp.int32) < n_active,
                          jnp.clip(j_end - j_base, 0, MOE_BLOCK), 0)
    block_cnt = ((block_cnt + GATHER_UNROLL - 1) // GATHER_UNROLL) * GATHER_UNROLL

    ys = _moe_ffn(h, block_e, block_first, block_next, block_wslot, block_cnt.astype(jnp.int32), row_tok,
                  n_active.reshape(1), row_w, w_in, w_out, layer)
    pos = dest.reshape(t // COMBINE_TM, COMBINE_TM, TOP_K).transpose(0, 2, 1).reshape(n_assign)
    return _combine(x, ys, pos, next_norm_w)


def kernel(x_prompt, x_sample, state_ret, state_gdn, state_conv, norm_mix, norm_ffn, norm_final,
           ret_w_in, ret_gn, ret_w_out, gdn_w_in, gdn_conv, gdn_a_log, gdn_dt_bias, gdn_norm,
           gdn_w_out, moe_w_group, moe_b_group, moe_w_expert, moe_b_expert, moe_w_in, moe_w_out):
    bp, lp, d = x_prompt.shape
    bs, ls, _ = x_sample.shape
    tp, ts = bp * lp, bs * ls
    x = (x_prompt.reshape(tp, d), x_sample.reshape(ts, d))

    h = _rmsnorm(x, norm_mix[0], BF16)
    ret_in = ret_w_in.shape[-1]
    proj = _matmul(h, ret_w_in, ret_in, MM_TM, MM_TN, layer=0)
    o_p, ret_p = _retention(proj, 0, bp, lp, 0, None, ret_gn[0],
                            chunk=RET_CHUNK, tb=RET_TB, bb=1)
    o_s, ret_s = _retention(proj, tp, bs, ls, PAST_LEN, state_ret[0], ret_gn[0],
                            chunk=ls, tb=ls, bb=RET_SAMPLE_BB)
    x = _matmul((o_p, o_s), ret_w_out, d, ROW_TILE, MM_TN_OUT, res=x, layer=0)
    x, h = _hier_moe(x, norm_ffn[0], moe_w_group[0], moe_b_group[0], moe_w_expert[0], moe_b_expert[0],
                     moe_w_in, moe_w_out, 0, next_norm_w=norm_mix[1])

    hv = GDN_V_HEADS * GDN_DV
    n_main = CONV_DIM + hv
    proj = _matmul(h, jnp.transpose(gdn_w_in[0]), n_main, MM_TM, MM_TN, w_is_nk=True)
    w_tail = jnp.zeros((d, LANES), F32).at[:, :2 * GDN_V_HEADS].set(gdn_w_in[0, :, n_main:])
    tail = _matmul(h, w_tail, LANES, MM_TM, LANES)
    t = tp + ts
    bt = tail[:, :GDN_V_HEADS].reshape(t, GDN_K_HEADS, 2)
    a = tail[:, GDN_V_HEADS:2 * GDN_V_HEADS].reshape(t, GDN_K_HEADS, 2)
    gates = jnp.concatenate([a, bt, jnp.zeros((t, GDN_K_HEADS, 4), F32)], axis=-1).transpose(1, 2, 0)
    conv0_pad = jnp.pad(state_conv[0], ((0, 0), (SUBLANES - (CONV_W - 1), 0), (0, 0)))
    o_p, gdn_p = _gdn(proj, gates, 0, bp, lp, None, None, gdn_conv[0], gdn_a_log[0], gdn_dt_bias[0],
                      gdn_norm[0], chunk=GDN_CHUNK, tb=GDN_TB, bb=1)
    o_s, gdn_s = _gdn(proj, gates, tp, bs, ls, conv0_pad, state_gdn[0], gdn_conv[0], gdn_a_log[0],
                      gdn_dt_bias[0], gdn_norm[0], chunk=ls, tb=ls, bb=GDN_SAMPLE_BB)
    proj8 = proj.reshape(t // SUBLANES, SUBLANES, n_main)
    keep = SUBLANES - (CONV_W - 1)
    conv_p = jnp.concatenate(
        [lax.slice(proj8, ((b * lp + lp) // SUBLANES - 1, keep, 0),
                   ((b * lp + lp) // SUBLANES, SUBLANES, CONV_DIM)) for b in range(bp)], axis=0)
    conv_s = lax.slice(proj8, ((tp + ls) // SUBLANES - 1, keep, 0), (t // SUBLANES, SUBLANES, CONV_DIM),
                       (ls // SUBLANES, 1, 1))
    x = _matmul((o_p, o_s), gdn_w_out, d, ROW_TILE, MM_TN_OUT, res=x, layer=0)
    x = _hier_moe(x, norm_ffn[1], moe_w_group[1], moe_b_group[1], moe_w_expert[1], moe_b_expert[1],
                  moe_w_in, moe_w_out, 1)

    y_p = _rmsnorm_rows(x, norm_final, F32, 0, tp)
    y_s = _rmsnorm_rows(x, norm_final, F32, tp, ts)
    return (y_p.reshape(bp, lp, d), y_s.reshape(bs, ls, d),
            ret_p[None], ret_s[None], gdn_p[None], gdn_s[None], conv_p[None], conv_s[None])
```

```python
import functools
import math

import numpy as np
import jax
import jax.numpy as jnp
from jax import lax
from jax.experimental import pallas as pl
from jax.experimental.pallas import tpu as pltpu

F32 = jnp.float32
BF16 = jnp.bfloat16

D_MODEL = 2048
PAST_LEN = 16384
RET_HEADS = 8
RET_DK = 256
RET_DV = 512
ROPE_BASE = 10000.0
GDN_K_HEADS = 16
GDN_V_HEADS = 32
GDN_DK = 128
GDN_DV = 128
CONV_W = 4
CONV_DIM = 2 * GDN_K_HEADS * GDN_DK + GDN_V_HEADS * GDN_DV
MOE_GROUPS = 8
EXPERTS_PER_GROUP = 8
N_EXPERTS = MOE_GROUPS * EXPERTS_PER_GROUP
TOP_K = 2
D_EXPERT = D_MODEL // 4
EPS = 1e-6

LANES = 128
SUBLANES = 8
BF16_ROWS = 16
VMEM_LIMIT_BYTES = 56 * 1024 * 1024

ROW_TILE = 512
MM_TM = 1024
MM_TN = 1024
MM_TN_OUT = 512
RET_CHUNK = 128
RET_TB = 2048
RET_SAMPLE_BB = 8
GDN_CHUNK = 64
GDN_TB = 1024
GDN_GROUP = 128
GDN_SAMPLE_BB = 32
MOE_BLOCK = 128
MOE_GATHER_SLOTS = 3
COMBINE_TM = 128
COMBINE_SLOTS = 3
GATHER_UNROLL = 8
MOE_W_SLABS = 4
BULK_DMA_PRIORITY = 1


def _params(sem):
    return pltpu.CompilerParams(dimension_semantics=sem, vmem_limit_bytes=VMEM_LIMIT_BYTES)


def _dot(a, b):
    return jnp.dot(a, b, preferred_element_type=F32)


def _dot_nt(a, b):
    return lax.dot_general(a, b, (((1,), (1,)), ((), ())), preferred_element_type=F32)


def _dot_tn(a, b):
    return lax.dot_general(a, b, (((0,), (0,)), ((), ())), preferred_element_type=F32)


def _split3(x):
    hi = x.astype(BF16)
    r1 = x - hi.astype(F32)
    mid = r1.astype(BF16)
    lo = (r1 - mid.astype(F32)).astype(BF16)
    return hi, mid, lo


def _silu(x):
    return x * (1.0 / (1.0 + jnp.exp(-x)))


def _row_segments(operands, tile):
    stacks = [op if isinstance(op, (tuple, list)) else (op,) for op in operands]
    cuts = set()
    for parts in stacks:
        edge = 0
        for p in parts:
            assert p.shape[0] % tile == 0
            cuts.add(edge)
            edge += p.shape[0] // tile
        cuts.add(edge)
    cuts = sorted(cuts)
    segments = [(a, b - a) for a, b in zip(cuts[:-1], cuts[1:])]
    sources = []
    for parts in stacks:
        assert sum(p.shape[0] for p in parts) == cuts[-1] * tile
        src, edge = [], 0
        for p in parts:
            for first, _ in segments:
                if edge <= first < edge + p.shape[0] // tile:
                    src.append((p, first - edge))
            edge += p.shape[0] // tile
        sources.append(src)
    return segments, sources


def _segment_spec(block, segment, source_tile0, row_axis, col_of):
    first, n_tiles = segment

    def index_map(*ids):
        return (source_tile0 + jnp.clip(ids[row_axis] - first, 0, n_tiles - 1), col_of(*ids))
    return pl.BlockSpec(block, index_map)


def _for_current_segment(segments, tile_id, fn):
    if len(segments) == 1:
        fn(0)
        return
    for k, (first, n_tiles) in enumerate(segments):
        pl.when(jnp.logical_and(tile_id >= first, tile_id < first + n_tiles))(functools.partial(fn, k))


def _rmsnorm_kernel(*refs, segments):
    x_refs, (w_ref, o_ref) = refs[:len(segments)], refs[len(segments):]

    def run(k):
        x = x_refs[k][...]
        ms = jnp.mean(x * x, axis=-1, keepdims=True)
        o_ref[...] = (x * lax.rsqrt(ms + EPS) * w_ref[...]).astype(o_ref.dtype)

    _for_current_segment(segments, pl.program_id(0), run)


def _rmsnorm(x, w, out_dtype):
    segments, (src,) = _row_segments([x], ROW_TILE)
    d = src[0][0].shape[1]
    n_tiles = segments[-1][0] + segments[-1][1]
    return pl.pallas_call(
        functools.partial(_rmsnorm_kernel, segments=segments),
        grid=(n_tiles,),
        in_specs=[_segment_spec((ROW_TILE, d), seg, t0, 0, lambda i: 0)
                  for seg, (_, t0) in zip(segments, src)]
        + [pl.BlockSpec((1, d), lambda i: (0, 0))],
        out_specs=pl.BlockSpec((ROW_TILE, d), lambda i: (i, 0)),
        out_shape=jax.ShapeDtypeStruct((n_tiles * ROW_TILE, d), out_dtype),
        compiler_params=_params(("arbitrary",)),
        name="rmsnorm",
    )(*[p for p, _ in src], w.reshape(1, d))


def _rmsnorm_rows(x, w, out_dtype, row0, n_rows):
    d = x.shape[1]
    assert row0 % ROW_TILE == 0 and n_rows % ROW_TILE == 0
    seg = (0, n_rows // ROW_TILE)
    return pl.pallas_call(
        functools.partial(_rmsnorm_kernel, segments=[seg]),
        grid=(n_rows // ROW_TILE,),
        in_specs=[_segment_spec((ROW_TILE, d), seg, row0 // ROW_TILE, 0, lambda i: 0),
                  pl.BlockSpec((1, d), lambda i: (0, 0))],
        out_specs=pl.BlockSpec((ROW_TILE, d), lambda i: (i, 0)),
        out_shape=jax.ShapeDtypeStruct((n_rows, d), out_dtype),
        compiler_params=_params(("arbitrary",)),
        name="rmsnorm",
    )(x, w.reshape(1, d))


def _matmul_kernel(*refs, segments, has_res, w_is_nk):
    n = len(segments)
    a_refs, w_ref = refs[:n], refs[n]
    r_refs = refs[n + 1:2 * n + 1] if has_res else None
    o_ref, wb_ref = refs[-2:]
    i = pl.program_id(1)

    @pl.when(i == 0)
    def _():
        wb_ref[...] = w_ref[...].astype(BF16)

    def run(k):
        acc = (_dot_nt if w_is_nk else _dot)(a_refs[k][...], wb_ref[...])
        o_ref[...] = acc + r_refs[k][...] if has_res else acc

    _for_current_segment(segments, i, run)


def _matmul(a, w, n_cols, tm, tn, res=None, layer=None, w_is_nk=False):
    segments, sources = _row_segments([a] if res is None else [a, res], tm)
    k = sources[0][0][0].shape[1]
    m_tiles = segments[-1][0] + segments[-1][1]
    in_specs = [_segment_spec((tm, k), seg, t0, 1, lambda j, i: 0)
                for seg, (_, t0) in zip(segments, sources[0])]
    if w_is_nk:
        in_specs.append(pl.BlockSpec((tn, k), lambda j, i: (j, 0)))
    elif layer is None:
        in_specs.append(pl.BlockSpec((k, tn), lambda j, i: (0, j)))
    else:
        in_specs.append(pl.BlockSpec((None, k, tn), lambda j, i: (layer, 0, j)))
    args = [p for p, _ in sources[0]] + [w]
    if res is not None:
        in_specs += [_segment_spec((tm, tn), seg, t0, 1, lambda j, i: j)
                     for seg, (_, t0) in zip(segments, sources[1])]
        args += [p for p, _ in sources[1]]
    return pl.pallas_call(
        functools.partial(_matmul_kernel, segments=segments, has_res=res is not None,
                          w_is_nk=w_is_nk),
        grid=(n_cols // tn, m_tiles),
        in_specs=in_specs,
        out_specs=pl.BlockSpec((tm, tn), lambda j, i: (i, j)),
        out_shape=jax.ShapeDtypeStruct((m_tiles * tm, n_cols), F32),
        scratch_shapes=[pltpu.VMEM((tn, k) if w_is_nk else (k, tn), BF16)],
        compiler_params=_params(("arbitrary", "arbitrary")),
        name="matmul",
    )(*args)


def _ret_kernel(lg_ref, q_ref, k_ref, v_ref, g_ref, cos_ref, sin_ref, gn_ref, *rest,
                chunk, n_seq, chunks_per_seq, has_s0):
    if has_s0:
        s0a_ref, s0b_ref, o_ref, s_ref = rest
    else:
        o_ref, s_ref = rest
    c = chunk
    h = pl.program_id(1)
    lg = lg_ref[h]

    if not has_s0:
        @pl.when(pl.program_id(2) == 0)
        def _():
            s_ref[...] = jnp.zeros(s_ref.shape, F32)

    ii = lax.broadcasted_iota(jnp.int32, (c, 1), 0).astype(F32)
    jj = lax.broadcasted_iota(jnp.int32, (1, c), 1).astype(F32)
    diff = ii - jj
    causal = diff >= 0.0
    decay = jnp.where(causal, jnp.exp(lg * jnp.where(causal, diff, 0.0)), 0.0)
    q_decay = jnp.exp(lg * (ii + 1.0))
    k_decay = jnp.exp(lg * (c - 1.0 - ii))
    chunk_decay = jnp.exp(jnp.full((1, 1), lg * c, F32))
    half = RET_DK // 2
    k_scale = RET_DK ** -0.5
    gn_w = gn_ref[...]
    op = (lambda t: t.astype(BF16)) if c % BF16_ROWS == 0 else (lambda t: t)
    outs = []

    for s in range(n_seq):
        state = jnp.concatenate([s0a_ref[s], s0b_ref[s]], axis=0) if has_s0 else s_ref[s]
        for ci in range(chunks_per_seq):
            r0 = (s * chunks_per_seq + ci) * c
            rows = slice(r0, r0 + c)
            cos = cos_ref[rows, :]
            sin = sin_ref[rows, :]
            q = q_ref[rows, :]
            k = k_ref[rows, :]
            q1, q2 = q[:, :half], q[:, half:]
            k1, k2 = k[:, :half], k[:, half:]
            qr = jnp.concatenate([q1 * cos - q2 * sin, q1 * sin + q2 * cos], axis=-1)
            kr = jnp.concatenate([k1 * cos - k2 * sin, k1 * sin + k2 * cos], axis=-1) * k_scale
            vb = op(v_ref[rows, :])
            qb = op(qr)
            scores = _dot_nt(qb, op(kr)) * decay
            inner = _dot(op(scores), vb)
            cross = _dot(qb, op(state)) * q_decay
            o = inner + cross
            state = state * chunk_decay + _dot_tn(op(kr * k_decay), vb)
            mu = jnp.mean(o, axis=-1, keepdims=True)
            oc = o - mu
            var = jnp.mean(oc * oc, axis=-1, keepdims=True)
            on = oc * lax.rsqrt(var + EPS) * gn_w
            res = on * _silu(g_ref[rows, :])
            if c % BF16_ROWS == 0:
                o_ref[rows, :] = res.astype(o_ref.dtype)
            else:
                outs.append(res)
        s_ref[s] = state
    if outs:
        o_ref[...] = jnp.concatenate(outs, axis=0).astype(o_ref.dtype)


def _retention(proj, row0, n_seq_total, seq_len, pos0, s0, gn_w, *, chunk, tb, bb):
    hq = RET_HEADS * RET_DK
    n_rows = n_seq_total * seq_len
    blk_rows = bb * tb
    n_inner = seq_len // tb
    n_outer = n_seq_total // bb
    assert row0 % blk_rows == 0 and seq_len % tb == 0 and tb % chunk == 0
    assert s0 is None or n_inner == 1
    rb0 = row0 // blk_rows
    half = RET_DK // 2

    pos = (pos0 + jnp.arange(seq_len)).astype(F32)
    inv = ROPE_BASE ** (-jnp.arange(half, dtype=F32) / half)
    ang = pos[:, None] * inv[None, :]
    cos = jnp.tile(jnp.cos(ang), (bb, 1)) if n_inner == 1 else jnp.cos(ang)
    sin = jnp.tile(jnp.sin(ang), (bb, 1)) if n_inner == 1 else jnp.sin(ang)
    log_gamma = jnp.log1p(-jnp.exp2(-5.0 - jnp.arange(RET_HEADS, dtype=F32)))

    def row_map(b, h, n):
        return rb0 + b * n_inner + n

    in_specs = [
        pl.BlockSpec(memory_space=pltpu.SMEM),
        pl.BlockSpec((blk_rows, RET_DK), lambda b, h, n: (row_map(b, h, n), h)),
        pl.BlockSpec((blk_rows, RET_DK), lambda b, h, n: (row_map(b, h, n), RET_HEADS + h)),
        pl.BlockSpec((blk_rows, RET_DV), lambda b, h, n: (row_map(b, h, n), 2 * hq // RET_DV + h)),
        pl.BlockSpec((blk_rows, RET_DV), lambda b, h, n: (row_map(b, h, n), 2 * hq // RET_DV + RET_HEADS + h)),
        pl.BlockSpec((blk_rows, half), lambda b, h, n: (n, 0)),
        pl.BlockSpec((blk_rows, half), lambda b, h, n: (n, 0)),
        pl.BlockSpec((None, 1, RET_DV), lambda b, h, n: (h, 0, 0)),
    ]
    args = [log_gamma, proj, proj, proj, proj, cos, sin, gn_w.reshape(RET_HEADS, 1, RET_DV)]
    if s0 is not None:
        in_specs += [pl.BlockSpec((bb, None, RET_DK // 2, RET_DV), lambda b, h, n: (b, h, 0, 0)),
                     pl.BlockSpec((bb, None, RET_DK // 2, RET_DV), lambda b, h, n: (b, h, 1, 0))]
        args += [s0, s0]
    out, s_fin = pl.pallas_call(
        functools.partial(_ret_kernel, chunk=chunk, n_seq=bb, chunks_per_seq=tb // chunk,
                          has_s0=s0 is not None),
        grid=(n_outer, RET_HEADS, n_inner),
        in_specs=in_specs,
        out_specs=[pl.BlockSpec((blk_rows, RET_DV), lambda b, h, n: (b * n_inner + n, h)),
                   pl.BlockSpec((bb, None, RET_DK, RET_DV), lambda b, h, n: (b, h, 0, 0))],
        out_shape=[jax.ShapeDtypeStruct((n_rows, RET_HEADS * RET_DV), BF16),
                   jax.ShapeDtypeStruct((n_seq_total, RET_HEADS, RET_DK, RET_DV), F32)],
        compiler_params=_params(("parallel", "parallel", "arbitrary")),
        name="retention",
    )(*args)
    return out, s_fin


def _shift_rows(x, prev8, s, x_ref=None):
    n = x.shape[0]
    first = x[:SUBLANES]
    row = lax.broadcasted_iota(jnp.int32, first.shape, 0)
    top = jnp.where(row < s, pltpu.roll(prev8, s, 0), pltpu.roll(first, s, 0))
    if n == SUBLANES:
        return top
    if x_ref is None:
        rest = pltpu.roll(x, s, 0)[SUBLANES:]
    else:
        rest = x_ref[pl.ds(SUBLANES - s, n - SUBLANES), :]
    return jnp.concatenate([top, rest], axis=0)


def _causal_conv_silu(x, prev8, cw, x_ref=None):
    acc = x * cw[CONV_W - 1:CONV_W, :]
    for s in range(1, CONV_W):
        acc = acc + _shift_rows(x, prev8, s, x_ref) * cw[CONV_W - 1 - s:CONV_W - s, :]
    return _silu(acc)


def _gdn_kernel(q_ref, k_ref, v_ref, z_ref, gb_ref, alog_ref, dtb_ref, cq_ref, ck_ref, cv_ref,
                nw_ref, *rest, chunk, group, n_seq, rows_per_seq, carry):
    has_init = not carry
    if has_init:
        pq_ref, pk_ref, pv_ref, s0_ref, o_ref, s_ref, prev_ref = rest
    else:
        o_ref, s_ref, prev_ref = rest
    c = chunk
    r = n_seq * rows_per_seq
    m = group
    dk = GDN_DK
    op = (lambda t: t.astype(BF16)) if c % BF16_ROWS == 0 else (lambda t: t)

    if carry:
        @pl.when(pl.program_id(2) == 0)
        def _():
            s_ref[...] = jnp.zeros(s_ref.shape, F32)
            prev_ref[...] = jnp.zeros(prev_ref.shape, F32)

    q_raw, k_raw, v_raw = q_ref[...], k_ref[...], v_ref[...]
    if carry:
        prev = prev_ref[...]
        qc = _causal_conv_silu(q_raw, prev[:, :dk], cq_ref[...], q_ref)
        kc = _causal_conv_silu(k_raw, prev[:, dk:2 * dk], ck_ref[...], k_ref)
        vc = _causal_conv_silu(v_raw, prev[:, 2 * dk:], cv_ref[...], v_ref)
        prev_ref[...] = jnp.concatenate(
            [q_raw[r - SUBLANES:], k_raw[r - SUBLANES:], v_raw[r - SUBLANES:]], axis=-1)
    else:
        qs, ks, vs = [], [], []
        for s in range(n_seq):
            rows = slice(s * rows_per_seq, (s + 1) * rows_per_seq)
            qs.append(_causal_conv_silu(q_raw[rows], pq_ref[s], cq_ref[...]))
            ks.append(_causal_conv_silu(k_raw[rows], pk_ref[s], ck_ref[...]))
            vs.append(_causal_conv_silu(v_raw[rows], pv_ref[s], cv_ref[...]))
        qc = jnp.concatenate(qs, axis=0)
        kc = jnp.concatenate(ks, axis=0)
        vc = jnp.concatenate(vs, axis=0)

    qn = qc * lax.rsqrt(jnp.sum(qc * qc, axis=-1, keepdims=True) + EPS) * (dk ** -0.5)
    kn = kc * lax.rsqrt(jnp.sum(kc * kc, axis=-1, keepdims=True) + EPS)
    qb = qn.astype(BF16)
    kb = kn.astype(BF16)

    gb = gb_ref[...]
    a_coef = jnp.exp(alog_ref[...][:, :1])
    xs = gb + dtb_ref[...][:, :1]
    softplus = jnp.maximum(xs, 0.0) + jnp.log(1.0 + jnp.exp(-jnp.abs(xs)))
    g_rows = -a_coef * softplus
    beta_rows = 1.0 / (1.0 + jnp.exp(-gb))
    rid = lax.broadcasted_iota(jnp.int32, gb.shape, 0)
    x8 = jnp.where(rid < 2, g_rows, beta_rows)

    ri = lax.broadcasted_iota(jnp.int32, (m, m), 0)
    ci = lax.broadcasted_iota(jnp.int32, (m, m), 1)
    log2c = int(math.log2(c))
    same = lax.shift_right_logical(ri, log2c) == lax.shift_right_logical(ci, log2c)
    tri_incl = jnp.logical_and(same, ri >= ci)
    tri_strict = jnp.logical_and(same, ri > ci)
    tri_b = jnp.where(tri_incl, 1.0, 0.0).astype(BF16)
    eye_b = jnp.where(ri == ci, 1.0, 0.0).astype(BF16)
    eye_f = jnp.where(ri == ci, 1.0, 0.0)
    n_sq = max(log2c - 1, 0)
    nw = nw_ref[...]

    units = [(g, vh) for g in range(r // m) for vh in range(2)]
    rows_of = {g: slice(g * m, (g + 1) * m) for g in range(r // m)}
    gate = {}
    for g in range(r // m):
        rows_g = rows_of[g]
        parts = _split3(x8[:, rows_g])
        cs_col = sum(_dot_nt(tri_b, p) for p in parts)
        cs_row = sum(_dot_nt(p, tri_b) for p in parts)
        x_col = sum(_dot_nt(eye_b, p) for p in parts)
        gate[g] = (cs_col, cs_row, x_col, jnp.exp(cs_col),
                   _dot_nt(kb[rows_g], kb[rows_g]), _dot_nt(qb[rows_g], kb[rows_g]))

    g_col, b_col, eg_col, dmat, t_mat, pw = {}, {}, {}, {}, {}, {}
    for u in units:
        g, vh = u
        cs_col, cs_row, x_col, exp_col, kk, _ = gate[g]
        g_col[u] = cs_col[:, vh:vh + 1]
        b_col[u] = x_col[:, 2 + vh:3 + vh]
        eg_col[u] = exp_col[:, vh:vh + 1]
        dmat[u] = jnp.exp(jnp.where(tri_incl, g_col[u] - cs_row[vh:vh + 1, :], -jnp.inf))
        pw[u] = jnp.where(tri_strict, -(kk * dmat[u]) * b_col[u], 0.0)
        t_mat[u] = eye_f + pw[u]
    for _ in range(n_sq):
        for u in units:
            pwb = pw[u].astype(BF16)
            pw[u] = _dot(pwb, pwb)
        for u in units:
            t_mat[u] = t_mat[u] + _dot(t_mat[u].astype(BF16), pw[u].astype(BF16))

    sol, q_in = {}, {}
    for u in units:
        g, vh = u
        v_h = vc[rows_of[g], vh * GDN_DV:(vh + 1) * GDN_DV]
        rhs = jnp.concatenate([v_h * b_col[u], kn[rows_of[g]] * (b_col[u] * eg_col[u])], axis=-1)
        sol[u] = _dot(t_mat[u].astype(BF16), rhs.astype(BF16))
        q_in[u] = qn[rows_of[g]] * eg_col[u]

    chunks = [(u, ch) for u in units for ch in range(m // c)]
    k_dec, g_last, qp = {}, {}, {}
    for uc in chunks:
        u, ch = uc
        rows = slice(ch * c, (ch + 1) * c)
        g_last[uc] = g_col[u][(ch + 1) * c - 1:(ch + 1) * c, :]
        k_dec[uc] = kn[rows_of[u[0]]][rows] * jnp.exp(g_last[uc] - g_col[u][rows])
        if carry:
            qp[uc] = _dot_tn(k_dec[uc].astype(BF16), sol[u][rows].astype(BF16))

    v_new, o_inter = {}, {}
    if carry:
        states = [s_ref[0, vh] for vh in range(2)]
        sb = {}
        for g in range(r // m):
            for ch in range(m // c):
                for vh in range(2):
                    uc = ((g, vh), ch)
                    sb[uc] = states[vh].astype(BF16)
                    states[vh] = (states[vh] * jnp.exp(g_last[uc])
                                  - _dot(qp[uc][:, GDN_DV:].astype(BF16), sb[uc]) + qp[uc][:, :GDN_DV])
        for vh in range(2):
            s_ref[0, vh] = states[vh]
        for uc in chunks:
            u, ch = uc
            rows = slice(ch * c, (ch + 1) * c)
            both = _dot(jnp.concatenate([sol[u][rows, GDN_DV:].astype(BF16),
                                         q_in[u][rows].astype(BF16)], axis=0), sb[uc])
            v_new[uc] = sol[u][rows, :GDN_DV] - both[:c]
            o_inter[uc] = both[c:]
    else:
        for uc in chunks:
            u, ch = uc
            rows = slice(ch * c, (ch + 1) * c)
            seq = (u[0] * m + ch * c) // rows_per_seq
            state = s0_ref[seq, u[1]]
            both = _dot(jnp.concatenate([sol[u][rows, GDN_DV:], q_in[u][rows]], axis=0).astype(BF16),
                        state.astype(BF16))
            v_new[uc] = sol[u][rows, :GDN_DV] - both[:c]
            o_inter[uc] = both[c:]
            s_ref[seq, u[1]] = (state * jnp.exp(g_last[uc])
                                + _dot_tn(op(k_dec[uc]), op(v_new[uc])))

    for u in units:
        g, vh = u
        v_new_all = jnp.concatenate([v_new[(u, ch)] for ch in range(m // c)], axis=0)
        o = jnp.concatenate([o_inter[(u, ch)] for ch in range(m // c)], axis=0)
        attn = gate[g][5] * dmat[u]
        o = o + _dot(attn.astype(BF16), v_new_all.astype(BF16))
        o = o * lax.rsqrt(jnp.mean(o * o, axis=-1, keepdims=True) + EPS) * nw
        z = z_ref[rows_of[g], vh * GDN_DV:(vh + 1) * GDN_DV]
        o_ref[rows_of[g], vh * GDN_DV:(vh + 1) * GDN_DV] = (o * _silu(z)).astype(o_ref.dtype)


def _gdn(proj, gates, row0, n_seq_total, seq_len, conv0_pad, s0, conv_w, a_log, dt_bias, norm_w,
         *, chunk, tb, bb):
    hk = GDN_K_HEADS * GDN_DK
    hv = GDN_V_HEADS * GDN_DV
    n_rows = n_seq_total * seq_len
    blk_rows = bb * tb
    n_inner = seq_len // tb
    n_outer = n_seq_total // bb
    carry = s0 is None
    assert row0 % blk_rows == 0 and seq_len % tb == 0 and tb % chunk == 0
    assert carry == (bb == 1) and (carry or (n_inner == 1 and tb == chunk))
    assert blk_rows % GDN_GROUP == 0 and GDN_GROUP % chunk == 0
    rb0 = row0 // blk_rows
    two = 2 * GDN_DV

    def row_map(b, h, n):
        return rb0 + b * n_inner + n

    alog8 = jnp.zeros((GDN_K_HEADS, SUBLANES, LANES), F32).at[:, :2, :].set(
        jnp.broadcast_to(a_log.reshape(GDN_K_HEADS, 2, 1), (GDN_K_HEADS, 2, LANES)))
    dtb8 = jnp.zeros((GDN_K_HEADS, SUBLANES, LANES), F32).at[:, :2, :].set(
        jnp.broadcast_to(dt_bias.reshape(GDN_K_HEADS, 2, 1), (GDN_K_HEADS, 2, LANES)))

    in_specs = [
        pl.BlockSpec((blk_rows, GDN_DK), lambda b, h, n: (row_map(b, h, n), h)),
        pl.BlockSpec((blk_rows, GDN_DK), lambda b, h, n: (row_map(b, h, n), GDN_K_HEADS + h)),
        pl.BlockSpec((blk_rows, two), lambda b, h, n: (row_map(b, h, n), 2 * hk // two + h)),
        pl.BlockSpec((blk_rows, two), lambda b, h, n: (row_map(b, h, n), CONV_DIM // two + h)),
        pl.BlockSpec((None, SUBLANES, blk_rows), lambda b, h, n: (h, 0, row_map(b, h, n))),
        pl.BlockSpec((None, SUBLANES, LANES), lambda b, h, n: (h, 0, 0)),
        pl.BlockSpec((None, SUBLANES, LANES), lambda b, h, n: (h, 0, 0)),
        pl.BlockSpec((CONV_W, GDN_DK), lambda b, h, n: (0, h)),
        pl.BlockSpec((CONV_W, GDN_DK), lambda b, h, n: (0, GDN_K_HEADS + h)),
        pl.BlockSpec((CONV_W, two), lambda b, h, n: (0, 2 * hk // two + h)),
        pl.BlockSpec((1, GDN_DV), lambda b, h, n: (0, 0)),
    ]
    args = [proj, proj, proj, proj, gates, alog8, dtb8, conv_w, conv_w, conv_w,
            norm_w.reshape(1, GDN_DV)]
    if not carry:
        in_specs += [
            pl.BlockSpec((bb, SUBLANES, GDN_DK), lambda b, h, n: (b, 0, h)),
            pl.BlockSpec((bb, SUBLANES, GDN_DK), lambda b, h, n: (b, 0, GDN_K_HEADS + h)),
            pl.BlockSpec((bb, SUBLANES, two), lambda b, h, n: (b, 0, 2 * hk // two + h)),
            pl.BlockSpec((bb, 2, GDN_DK, GDN_DV), lambda b, h, n: (b, h, 0, 0)),
        ]
        args += [conv0_pad, conv0_pad, conv0_pad, s0]
    out, s_fin = pl.pallas_call(
        functools.partial(_gdn_kernel, chunk=chunk, group=GDN_GROUP, n_seq=bb, rows_per_seq=tb,
                          carry=carry),
        grid=(n_outer, GDN_K_HEADS, n_inner),
        in_specs=in_specs,
        out_specs=[pl.BlockSpec((blk_rows, two), lambda b, h, n: (b * n_inner + n, h)),
                   pl.BlockSpec((bb, 2, GDN_DK, GDN_DV), lambda b, h, n: (b, h, 0, 0))],
        out_shape=[jax.ShapeDtypeStruct((n_rows, hv), BF16),
                   jax.ShapeDtypeStruct((n_seq_total, GDN_V_HEADS, GDN_DK, GDN_DV), F32)],
        scratch_shapes=[pltpu.VMEM((SUBLANES, 2 * GDN_DK + two), F32)],
        compiler_params=_params(("parallel", "parallel", "arbitrary")),
        name="gated_delta",
    )(*args)
    return out, s_fin


def _router_kernel(x_ref, w_ref, wr_ref, br_ref, h_ref, lg_ref):
    x = x_ref[...]
    ms = jnp.mean(x * x, axis=-1, keepdims=True)
    h = x * lax.rsqrt(ms + EPS) * w_ref[...]
    h_ref[...] = h
    lg_ref[...] = _dot_nt(wr_ref[...].astype(BF16), h.astype(BF16)) + br_ref[...]


def _router(x, norm_w, w_route, b_route):
    t, d = x.shape
    return pl.pallas_call(
        _router_kernel,
        grid=(t // ROW_TILE,),
        in_specs=[pl.BlockSpec((ROW_TILE, d), lambda i: (i, 0)),
                  pl.BlockSpec((1, d), lambda i: (0, 0)),
                  pl.BlockSpec((LANES, d), lambda i: (0, 0)),
                  pl.BlockSpec((LANES, 1), lambda i: (0, 0))],
        out_specs=[pl.BlockSpec((ROW_TILE, d), lambda i: (i, 0)),
                   pl.BlockSpec((LANES, ROW_TILE), lambda i: (0, i))],
        out_shape=[jax.ShapeDtypeStruct((t, d), F32),
                   jax.ShapeDtypeStruct((LANES, t), F32)],
        compiler_params=_params(("parallel",)),
        name="moe_router",
    )(x, norm_w.reshape(1, d), w_route, b_route)


def _row_gather_start(idx_ref, base, n, src_hbm, dst_ref, sem):
    def body(r, carry):
        tok = idx_ref[base + r]
        pltpu.make_async_copy(src_hbm.at[pl.ds(tok, 1)], dst_ref.at[pl.ds(r, 1)], sem).start()
        return carry
    lax.fori_loop(0, n, body, 0, unroll=GATHER_UNROLL)


def _row_gather_wait(n, src_hbm, dst_ref, sem):
    pltpu.make_async_copy(src_hbm.at[pl.ds(0, n)], dst_ref, sem).wait()


def _row_gather_start_counted(idx_ref, base, n, src_hbm, dst_ref, sem):
    def body(g, carry):
        for k in range(GATHER_UNROLL):
            r = g * GATHER_UNROLL + k
            tok = idx_ref[base + r]
            pltpu.make_async_copy(src_hbm.at[pl.ds(tok, 1)], dst_ref.at[pl.ds(r, 1)], sem).start()
        return carry
    lax.fori_loop(0, n // GATHER_UNROLL, body, 0)


def _row_gather_wait_counted(n, src_hbm, dst_ref, sem):
    @pl.when(n > 0)
    def _():
        rows = pl.multiple_of(n, GATHER_UNROLL)
        pltpu.make_async_copy(src_hbm.at[pl.ds(0, rows)], dst_ref.at[pl.ds(0, rows)], sem).wait()


def _moe_ffn_kernel(be_ref, first_ref, nxt_ref, ws_ref, cnt_ref, rt_ref, na_ref, h_hbm, rw_ref,
                    win_hbm, wout_hbm, o_ref, xbuf, sem, win_f, wout_f, win_sem, wout_sem, winb, woutb,
                    *, layer):
    i = pl.program_id(0)
    n_active = na_ref[0]
    slot = lax.rem(i, MOE_GATHER_SLOTS)

    def gather_start(blk):
        s = lax.rem(blk, MOE_GATHER_SLOTS)
        _row_gather_start_counted(rt_ref, blk * MOE_BLOCK, cnt_ref[blk], h_hbm, xbuf.at[s], sem.at[s])

    @pl.when(i == 0)
    def _():
        xbuf[...] = jnp.zeros(xbuf.shape, F32)

    def weight_copies(e, ws):
        r_in = win_f.shape[1] // MOE_W_SLABS
        r_out = wout_f.shape[1] // MOE_W_SLABS
        cps = []
        for k in range(MOE_W_SLABS):
            cps.append(pltpu.make_async_copy(win_hbm.at[layer, e, pl.ds(k * r_in, r_in)],
                                             win_f.at[ws, pl.ds(k * r_in, r_in)], win_sem.at[ws, k]))
            cps.append(pltpu.make_async_copy(wout_hbm.at[layer, e, pl.ds(k * r_out, r_out)],
                                             wout_f.at[ws, pl.ds(k * r_out, r_out)], wout_sem.at[ws, k]))
        return cps

    @pl.when(jnp.logical_and(i == 0, n_active > 0))
    def _():
        for cp in weight_copies(be_ref[0], ws_ref[0]):
            cp.start(priority=BULK_DMA_PRIORITY)
        for blk in range(MOE_GATHER_SLOTS - 1):
            pl.when(blk < n_active)(functools.partial(gather_start, blk))

    @pl.when(i + MOE_GATHER_SLOTS - 1 < n_active)
    def _():
        gather_start(i + MOE_GATHER_SLOTS - 1)

    @pl.when(i < n_active)
    def _():
        ws = ws_ref[i]

        @pl.when(first_ref[i] == 1)
        def _():
            @pl.when(nxt_ref[i] >= 0)
            def _():
                for cp in weight_copies(nxt_ref[i], 1 - ws):
                    cp.start(priority=BULK_DMA_PRIORITY)

            for cp in weight_copies(be_ref[i], ws):
                cp.wait()
            winb[...] = win_f[ws].astype(BF16)
            woutb[...] = wout_f[ws].astype(BF16)

        _row_gather_wait_counted(cnt_ref[i], h_hbm, xbuf.at[slot], sem.at[slot])
        x = xbuf[slot].astype(BF16)
        mid = _dot(x, winb[...])
        act = (_silu(mid[:, :D_EXPERT]) * mid[:, D_EXPERT:]).astype(BF16)
        o_ref[...] = _dot(act, woutb[...]) * rw_ref[...]

    @pl.when(i >= n_active)
    def _():
        o_ref[...] = jnp.zeros(o_ref.shape, F32)


def _moe_ffn(h, block_e, block_first, block_next, block_wslot, block_cnt, row_tok, n_active, row_w,
             w_in, w_out, layer):
    t, d = h.shape
    n_blocks = block_e.shape[0]
    n_rows = n_blocks * MOE_BLOCK
    grid_spec = pltpu.PrefetchScalarGridSpec(
        num_scalar_prefetch=7,
        grid=(n_blocks,),
        in_specs=[
            pl.BlockSpec(memory_space=pl.ANY),
            pl.BlockSpec((MOE_BLOCK, 1), lambda i, *_: (i, 0)),
            pl.BlockSpec(memory_space=pl.ANY),
            pl.BlockSpec(memory_space=pl.ANY),
        ],
        out_specs=pl.BlockSpec((MOE_BLOCK, d), lambda i, *_: (i, 0)),
        scratch_shapes=[pltpu.VMEM((MOE_GATHER_SLOTS, MOE_BLOCK, d), F32),
                        pltpu.SemaphoreType.DMA((MOE_GATHER_SLOTS,)),
                        pltpu.VMEM((2, d, 2 * D_EXPERT), F32),
                        pltpu.VMEM((2, D_EXPERT, d), F32),
                        pltpu.SemaphoreType.DMA((2, MOE_W_SLABS)),
                        pltpu.SemaphoreType.DMA((2, MOE_W_SLABS)),
                        pltpu.VMEM((d, 2 * D_EXPERT), BF16),
                        pltpu.VMEM((D_EXPERT, d), BF16)],
    )
    return pl.pallas_call(
        functools.partial(_moe_ffn_kernel, layer=layer),
        grid_spec=grid_spec,
        out_shape=jax.ShapeDtypeStruct((n_rows, d), F32),
        compiler_params=_params(("arbitrary",)),
        name="moe_ffn",
    )(block_e, block_first, block_next, block_wslot, block_cnt, row_tok, n_active, h,
      row_w.reshape(n_rows, 1), w_in, w_out)


def _combine_kernel(pos_ref, ys_hbm, x_ref, *rest, with_norm, final_split):
    if final_split is not None:
        nw_ref, y1_ref, y2_ref, buf, sem = rest
    elif with_norm:
        nw_ref, o_ref, hn_ref, buf, sem = rest
    else:
        o_ref, buf, sem = rest
    i = pl.program_id(0)
    n = pl.num_programs(0)
    slot = lax.rem(i, COMBINE_SLOTS)
    rows = 2 * COMBINE_TM

    def gather_start(tile):
        s = lax.rem(tile, COMBINE_SLOTS)
        _row_gather_start(pos_ref, tile * rows, rows, ys_hbm, buf.at[s], sem.at[s])

    @pl.when(i == 0)
    def _():
        for tile in range(COMBINE_SLOTS - 1):
            pl.when(tile < n)(functools.partial(gather_start, tile))

    @pl.when(i + COMBINE_SLOTS - 1 < n)
    def _():
        gather_start(i + COMBINE_SLOTS - 1)

    _row_gather_wait(rows, ys_hbm, buf.at[slot], sem.at[slot])
    y = x_ref[...] + buf[slot, :COMBINE_TM, :] + buf[slot, COMBINE_TM:, :]
    if final_split is None:
        o_ref[...] = y
    if with_norm or final_split is not None:
        ms = jnp.mean(y * y, axis=-1, keepdims=True)
        yn = y * lax.rsqrt(ms + EPS) * nw_ref[...]
    if with_norm:
        hn_ref[...] = yn.astype(hn_ref.dtype)
    if final_split is not None:
        @pl.when(i < final_split)
        def _():
            y1_ref[...] = yn

        @pl.when(i >= final_split)
        def _():
            y2_ref[...] = yn


def _combine(x, ys, pos, next_norm_w=None, final=None):
    t, d = x.shape
    with_norm = next_norm_w is not None
    row_spec = pl.BlockSpec((COMBINE_TM, d), lambda i, p: (i, 0))
    in_specs = [pl.BlockSpec(memory_space=pl.ANY), row_spec]
    args = [pos, ys, x]
    out_specs, out_shape = row_spec, jax.ShapeDtypeStruct((t, d), F32)
    final_split = None
    if final is not None:
        norm_w, n_first = final
        assert not with_norm and n_first % COMBINE_TM == 0 and 0 < n_first < t
        final_split = n_first // COMBINE_TM
        in_specs.append(pl.BlockSpec((1, d), lambda i, p: (0, 0)))
        args.append(norm_w.reshape(1, d))
        out_specs = [pl.BlockSpec((COMBINE_TM, d), lambda i, p: (jnp.minimum(i, final_split - 1), 0)),
                     pl.BlockSpec((COMBINE_TM, d), lambda i, p: (jnp.maximum(i - final_split, 0), 0))]
        out_shape = [jax.ShapeDtypeStruct((n_first, d), F32), jax.ShapeDtypeStruct((t - n_first, d), F32)]
    if with_norm:
        in_specs.append(pl.BlockSpec((1, d), lambda i, p: (0, 0)))
        args.append(next_norm_w.reshape(1, d))
        out_specs = [row_spec, row_spec]
        out_shape = [out_shape, jax.ShapeDtypeStruct((t, d), BF16)]
    grid_spec = pltpu.PrefetchScalarGridSpec(
        num_scalar_prefetch=1,
        grid=(t // COMBINE_TM,),
        in_specs=in_specs,
        out_specs=out_specs,
        scratch_shapes=[pltpu.VMEM((COMBINE_SLOTS, 2 * COMBINE_TM, d), F32),
                        pltpu.SemaphoreType.DMA((COMBINE_SLOTS,))],
    )
    return pl.pallas_call(
        functools.partial(_combine_kernel, with_norm=with_norm, final_split=final_split),
        grid_spec=grid_spec,
        out_shape=out_shape,
        compiler_params=_params(("arbitrary",)),
        name="moe_combine",
    )(*args)


def _hier_moe(x, norm_w, w_group, b_group, w_expert, b_expert, w_in, w_out, layer, next_norm_w=None,
              final=None):
    t, d = x.shape
    n_route = MOE_GROUPS + N_EXPERTS
    w_route = jnp.zeros((LANES, d), F32).at[:MOE_GROUPS].set(w_group.T).at[MOE_GROUPS:n_route].set(w_expert.T)
    b_route = jnp.zeros((LANES, 1), F32).at[:MOE_GROUPS, 0].set(b_group).at[MOE_GROUPS:n_route, 0].set(b_expert)
    h, logits = _router(x, norm_w, w_route, b_route)

    g_logits = logits[:MOE_GROUPS]
    g_max = jnp.max(g_logits, axis=0)
    g_sel = jnp.argmax(g_logits, axis=0).astype(jnp.int32)
    g_w = 1.0 / jnp.sum(jnp.exp(g_logits - g_max[None, :]), axis=0)
    e_all = logits[MOE_GROUPS:n_route].reshape(MOE_GROUPS, EXPERTS_PER_GROUP, t)
    grp = jnp.arange(MOE_GROUPS, dtype=jnp.int32)[:, None, None]
    e_logits = jnp.sum(jnp.where(grp == g_sel[None, None, :], e_all, 0.0), axis=0)
    slot_id = jnp.arange(EXPERTS_PER_GROUP, dtype=jnp.int32)[:, None]
    i1 = jnp.argmax(e_logits, axis=0).astype(jnp.int32)
    v1 = jnp.max(e_logits, axis=0)
    rest = jnp.where(slot_id == i1[None, :], -jnp.inf, e_logits)
    i2 = jnp.argmax(rest, axis=0).astype(jnp.int32)
    v2 = jnp.max(rest, axis=0)
    e21 = jnp.exp(v2 - v1)
    weights = jnp.stack([g_w / (1.0 + e21), g_w * e21 / (1.0 + e21)], axis=1)
    expert_id = jnp.stack([g_sel * EXPERTS_PER_GROUP + i1, g_sel * EXPERTS_PER_GROUP + i2], axis=1)

    n_assign = t * TOP_K
    flat_e = expert_id.reshape(n_assign)
    iota = jnp.arange(n_assign, dtype=jnp.int32)
    experts = jnp.arange(N_EXPERTS, dtype=jnp.int32)
    sorted_e, order, w_sorted = lax.sort((flat_e, iota, weights.reshape(n_assign)), num_keys=1,
                                         is_stable=True)
    counts = jnp.sum((flat_e[:, None] == experts[None, :]).astype(jnp.int32), axis=0)
    start = jnp.cumsum(counts) - counts
    nblk_e = (counts + MOE_BLOCK - 1) // MOE_BLOCK
    blk_end = jnp.cumsum(nblk_e)
    blk_start = blk_end - nblk_e

    def lookup(idx, table):
        return jnp.sum(jnp.where(idx[:, None] == experts[None, :], table[None, :], 0), axis=1)

    dest_sorted = iota + lookup(sorted_e, blk_start * MOE_BLOCK - start)
    _, dest = lax.sort((order, dest_sorted), num_keys=1)

    n_blocks = -(-n_assign // MOE_BLOCK) + N_EXPERTS
    n_active = blk_end[-1].astype(jnp.int32)
    blk = jnp.minimum(jnp.arange(n_blocks, dtype=jnp.int32), jnp.maximum(n_active - 1, 0))
    block_e = jnp.minimum(jnp.sum((blk[:, None] >= blk_end[None, :]).astype(jnp.int32), axis=1),
                          N_EXPERTS - 1)
    block_first = (blk == lookup(block_e, blk_start)).astype(jnp.int32)
    active = counts > 0
    ordinal = jnp.cumsum(active.astype(jnp.int32)) - 1
    block_wslot = lookup(block_e, ordinal) % 2
    later = lax.cummin(jnp.where(active, experts, N_EXPERTS), reverse=True)
    next_e = jnp.concatenate([later[1:], jnp.full((1,), N_EXPERTS, jnp.int32)])
    block_next = lookup(block_e, jnp.where(next_e >= N_EXPERTS, -1, next_e))

    j_base = lookup(block_e, start) + (blk - lookup(block_e, blk_start)) * MOE_BLOCK
    j_end = lookup(block_e, start + counts)
    j = j_base[:, None] + jnp.arange(MOE_BLOCK, dtype=jnp.int32)[None, :]
    valid = jnp.logical_and(j < j_end[:, None],
                            jnp.arange(n_blocks, dtype=jnp.int32)[:, None] < n_active)
    j = jnp.clip(j, 0, n_assign - 1).reshape(-1)
    valid = valid.reshape(-1)
    row_tok = jnp.where(valid, order[j] // TOP_K, 0)
    row_w = jnp.where(valid, w_sorted[j], 0.0)
    block_cnt = jnp.where(jnp.arange(n_blocks, dtype=jnp.int32) < n_active,
                          jnp.clip(j_end - j_base, 0, MOE_BLOCK), 0)
    block_cnt = ((block_cnt + GATHER_UNROLL - 1) // GATHER_UNROLL) * GATHER_UNROLL

    ys = _moe_ffn(h, block_e, block_first, block_next, block_wslot, block_cnt.astype(jnp.int32), row_tok,
                  n_active.reshape(1), row_w, w_in, w_out, layer)
    pos = dest.reshape(t // COMBINE_TM, COMBINE_TM, TOP_K).transpose(0, 2, 1).reshape(n_assign)
    return _combine(x, ys, pos, next_norm_w, final)


def kernel(x_prompt, x_sample, state_ret, state_gdn, state_conv, norm_mix, norm_ffn, norm_final,
           ret_w_in, ret_gn, ret_w_out, gdn_w_in, gdn_conv, gdn_a_log, gdn_dt_bias, gdn_norm,
           gdn_w_out, moe_w_group, moe_b_group, moe_w_expert, moe_b_expert, moe_w_in, moe_w_out):
    bp, lp, d = x_prompt.shape
    bs, ls, _ = x_sample.shape
    tp, ts = bp * lp, bs * ls
    x = (x_prompt.reshape(tp, d), x_sample.reshape(ts, d))

    h = _rmsnorm(x, norm_mix[0], BF16)
    ret_in = ret_w_in.shape[-1]
    proj = _matmul(h, ret_w_in, ret_in, MM_TM, MM_TN, layer=0)
    o_p, ret_p = _retention(proj, 0, bp, lp, 0, None, ret_gn[0],
                            chunk=RET_CHUNK, tb=RET_TB, bb=1)
    o_s, ret_s = _retention(proj, tp, bs, ls, PAST_LEN, state_ret[0], ret_gn[0],
                            chunk=ls, tb=ls, bb=RET_SAMPLE_BB)
    x = _matmul((o_p, o_s), ret_w_out, d, ROW_TILE, MM_TN_OUT, res=x, layer=0)
    x, h = _hier_moe(x, norm_ffn[0], moe_w_group[0], moe_b_group[0], moe_w_expert[0], moe_b_expert[0],
                     moe_w_in, moe_w_out, 0, next_norm_w=norm_mix[1])

    hv = GDN_V_HEADS * GDN_DV
    n_main = CONV_DIM + hv
    proj = _matmul(h, jnp.transpose(gdn_w_in[0]), n_main, MM_TM, MM_TN, w_is_nk=True)
    w_tail = jnp.zeros((d, LANES), F32).at[:, :2 * GDN_V_HEADS].set(gdn_w_in[0, :, n_main:])
    tail = _matmul(h, w_tail, LANES, MM_TM, LANES)
    t = tp + ts
    bt = tail[:, :GDN_V_HEADS].reshape(t, GDN_K_HEADS, 2)
    a = tail[:, GDN_V_HEADS:2 * GDN_V_HEADS].reshape(t, GDN_K_HEADS, 2)
    gates = jnp.concatenate([a, bt, jnp.zeros((t, GDN_K_HEADS, 4), F32)], axis=-1).transpose(1, 2, 0)
    conv0_pad = jnp.pad(state_conv[0], ((0, 0), (SUBLANES - (CONV_W - 1), 0), (0, 0)))
    o_p, gdn_p = _gdn(proj, gates, 0, bp, lp, None, None, gdn_conv[0], gdn_a_log[0], gdn_dt_bias[0],
                      gdn_norm[0], chunk=GDN_CHUNK, tb=GDN_TB, bb=1)
    o_s, gdn_s = _gdn(proj, gates, tp, bs, ls, conv0_pad, state_gdn[0], gdn_conv[0], gdn_a_log[0],
                      gdn_dt_bias[0], gdn_norm[0], chunk=ls, tb=ls, bb=GDN_SAMPLE_BB)
    proj8 = proj.reshape(t // SUBLANES, SUBLANES, n_main)
    keep = SUBLANES - (CONV_W - 1)
    conv_p = jnp.concatenate(
        [lax.slice(proj8, ((b * lp + lp) // SUBLANES - 1, keep, 0),
                   ((b * lp + lp) // SUBLANES, SUBLANES, CONV_DIM)) for b in range(bp)], axis=0)
    conv_s = lax.slice(proj8, ((tp + ls) // SUBLANES - 1, keep, 0), (t // SUBLANES, SUBLANES, CONV_DIM),
                       (ls // SUBLANES, 1, 1))
    x = _matmul((o_p, o_s), gdn_w_out, d, ROW_TILE, MM_TN_OUT, res=x, layer=0)
    y_p, y_s = _hier_moe(x, norm_ffn[1], moe_w_group[1], moe_b_group[1], moe_w_expert[1], moe_b_expert[1],
                         moe_w_in, moe_w_out, 1, final=(norm_final, tp))
    return (y_p.reshape(bp, lp, d), y_s.reshape(bs, ls, d),
            ret_p[None], ret_s[None], gdn_p[None], gdn_s[None], conv_p[None], conv_s[None])
```

```python
import functools
import math

import numpy as np
import jax
import jax.numpy as jnp
from jax import lax
from jax.experimental import pallas as pl
from jax.experimental.pallas import tpu as pltpu

F32 = jnp.float32
BF16 = jnp.bfloat16

D_MODEL = 2048
PAST_LEN = 16384
RET_HEADS = 8
RET_DK = 256
RET_DV = 512
ROPE_BASE = 10000.0
GDN_K_HEADS = 16
GDN_V_HEADS = 32
GDN_DK = 128
GDN_DV = 128
CONV_W = 4
CONV_DIM = 2 * GDN_K_HEADS * GDN_DK + GDN_V_HEADS * GDN_DV
MOE_GROUPS = 8
EXPERTS_PER_GROUP = 8
N_EXPERTS = MOE_GROUPS * EXPERTS_PER_GROUP
TOP_K = 2
D_EXPERT = D_MODEL // 4
EPS = 1e-6

LANES = 128
SUBLANES = 8
BF16_ROWS = 16
VMEM_LIMIT_BYTES = 56 * 1024 * 1024

ROW_TILE = 512
MM_TM = 1024
MM_TN = 1024
MM_TN_OUT = 512
RET_CHUNK = 128
RET_TB = 2048
RET_SAMPLE_BB = 8
GDN_CHUNK = 64
GDN_TB = 1024
GDN_GROUP = 128
GDN_SAMPLE_BB = 32
MOE_BLOCK = 128
MOE_GATHER_SLOTS = 3
COMBINE_TM = 128
COMBINE_SLOTS = 3
GATHER_UNROLL = 8
MOE_W_SLABS = 4
BULK_DMA_PRIORITY = 1


def _params(sem):
    return pltpu.CompilerParams(dimension_semantics=sem, vmem_limit_bytes=VMEM_LIMIT_BYTES)


def _dot(a, b):
    return jnp.dot(a, b, preferred_element_type=F32)


def _dot_nt(a, b):
    return lax.dot_general(a, b, (((1,), (1,)), ((), ())), preferred_element_type=F32)


def _dot_tn(a, b):
    return lax.dot_general(a, b, (((0,), (0,)), ((), ())), preferred_element_type=F32)


def _split3(x):
    hi = x.astype(BF16)
    r1 = x - hi.astype(F32)
    mid = r1.astype(BF16)
    lo = (r1 - mid.astype(F32)).astype(BF16)
    return hi, mid, lo


def _silu(x):
    return x * (1.0 / (1.0 + jnp.exp(-x)))


def _row_segments(operands, tile):
    stacks = [op if isinstance(op, (tuple, list)) else (op,) for op in operands]
    cuts = set()
    for parts in stacks:
        edge = 0
        for p in parts:
            assert p.shape[0] % tile == 0
            cuts.add(edge)
            edge += p.shape[0] // tile
        cuts.add(edge)
    cuts = sorted(cuts)
    segments = [(a, b - a) for a, b in zip(cuts[:-1], cuts[1:])]
    sources = []
    for parts in stacks:
        assert sum(p.shape[0] for p in parts) == cuts[-1] * tile
        src, edge = [], 0
        for p in parts:
            for first, _ in segments:
                if edge <= first < edge + p.shape[0] // tile:
                    src.append((p, first - edge))
            edge += p.shape[0] // tile
        sources.append(src)
    return segments, sources


def _segment_spec(block, segment, source_tile0, row_axis, col_of):
    first, n_tiles = segment

    def index_map(*ids):
        return (source_tile0 + jnp.clip(ids[row_axis] - first, 0, n_tiles - 1), col_of(*ids))
    return pl.BlockSpec(block, index_map)


def _for_current_segment(segments, tile_id, fn):
    if len(segments) == 1:
        fn(0)
        return
    for k, (first, n_tiles) in enumerate(segments):
        pl.when(jnp.logical_and(tile_id >= first, tile_id < first + n_tiles))(functools.partial(fn, k))


def _rmsnorm_kernel(*refs, segments):
    x_refs, (w_ref, o_ref) = refs[:len(segments)], refs[len(segments):]

    def run(k):
        x = x_refs[k][...]
        ms = jnp.mean(x * x, axis=-1, keepdims=True)
        o_ref[...] = (x * lax.rsqrt(ms + EPS) * w_ref[...]).astype(o_ref.dtype)

    _for_current_segment(segments, pl.program_id(0), run)


def _rmsnorm(x, w, out_dtype):
    segments, (src,) = _row_segments([x], ROW_TILE)
    d = src[0][0].shape[1]
    n_tiles = segments[-1][0] + segments[-1][1]
    return pl.pallas_call(
        functools.partial(_rmsnorm_kernel, segments=segments),
        grid=(n_tiles,),
        in_specs=[_segment_spec((ROW_TILE, d), seg, t0, 0, lambda i: 0)
                  for seg, (_, t0) in zip(segments, src)]
        + [pl.BlockSpec((1, d), lambda i: (0, 0))],
        out_specs=pl.BlockSpec((ROW_TILE, d), lambda i: (i, 0)),
        out_shape=jax.ShapeDtypeStruct((n_tiles * ROW_TILE, d), out_dtype),
        compiler_params=_params(("arbitrary",)),
        name="rmsnorm",
    )(*[p for p, _ in src], w.reshape(1, d))


def _rmsnorm_rows(x, w, out_dtype, row0, n_rows):
    d = x.shape[1]
    assert row0 % ROW_TILE == 0 and n_rows % ROW_TILE == 0
    seg = (0, n_rows // ROW_TILE)
    return pl.pallas_call(
        functools.partial(_rmsnorm_kernel, segments=[seg]),
        grid=(n_rows // ROW_TILE,),
        in_specs=[_segment_spec((ROW_TILE, d), seg, row0 // ROW_TILE, 0, lambda i: 0),
                  pl.BlockSpec((1, d), lambda i: (0, 0))],
        out_specs=pl.BlockSpec((ROW_TILE, d), lambda i: (i, 0)),
        out_shape=jax.ShapeDtypeStruct((n_rows, d), out_dtype),
        compiler_params=_params(("arbitrary",)),
        name="rmsnorm",
    )(x, w.reshape(1, d))


def _matmul_kernel(*refs, segments, has_res, w_is_nk):
    n = len(segments)
    a_refs, w_ref = refs[:n], refs[n]
    r_refs = refs[n + 1:2 * n + 1] if has_res else None
    o_ref, wb_ref = refs[-2:]
    i = pl.program_id(1)

    @pl.when(i == 0)
    def _():
        wb_ref[...] = w_ref[...].astype(BF16)

    def run(k):
        acc = (_dot_nt if w_is_nk else _dot)(a_refs[k][...], wb_ref[...])
        o_ref[...] = acc + r_refs[k][...] if has_res else acc

    _for_current_segment(segments, i, run)


def _matmul(a, w, n_cols, tm, tn, res=None, layer=None, w_is_nk=False):
    segments, sources = _row_segments([a] if res is None else [a, res], tm)
    k = sources[0][0][0].shape[1]
    m_tiles = segments[-1][0] + segments[-1][1]
    in_specs = [_segment_spec((tm, k), seg, t0, 1, lambda j, i: 0)
                for seg, (_, t0) in zip(segments, sources[0])]
    if w_is_nk:
        in_specs.append(pl.BlockSpec((tn, k), lambda j, i: (j, 0)))
    elif layer is None:
        in_specs.append(pl.BlockSpec((k, tn), lambda j, i: (0, j)))
    else:
        in_specs.append(pl.BlockSpec((None, k, tn), lambda j, i: (layer, 0, j)))
    args = [p for p, _ in sources[0]] + [w]
    if res is not None:
        in_specs += [_segment_spec((tm, tn), seg, t0, 1, lambda j, i: j)
                     for seg, (_, t0) in zip(segments, sources[1])]
        args += [p for p, _ in sources[1]]
    return pl.pallas_call(
        functools.partial(_matmul_kernel, segments=segments, has_res=res is not None,
                          w_is_nk=w_is_nk),
        grid=(n_cols // tn, m_tiles),
        in_specs=in_specs,
        out_specs=pl.BlockSpec((tm, tn), lambda j, i: (i, j)),
        out_shape=jax.ShapeDtypeStruct((m_tiles * tm, n_cols), F32),
        scratch_shapes=[pltpu.VMEM((tn, k) if w_is_nk else (k, tn), BF16)],
        compiler_params=_params(("arbitrary", "arbitrary")),
        name="matmul",
    )(*args)


def _ret_kernel(lg_ref, q_ref, k_ref, v_ref, g_ref, cos_ref, sin_ref, gn_ref, *rest,
                chunk, n_seq, chunks_per_seq, has_s0):
    if has_s0:
        s0a_ref, s0b_ref, o_ref, s_ref = rest
    else:
        o_ref, s_ref = rest
    c = chunk
    h = pl.program_id(1)
    lg = lg_ref[h]

    if not has_s0:
        @pl.when(pl.program_id(2) == 0)
        def _():
            s_ref[...] = jnp.zeros(s_ref.shape, F32)

    ii = lax.broadcasted_iota(jnp.int32, (c, 1), 0).astype(F32)
    jj = lax.broadcasted_iota(jnp.int32, (1, c), 1).astype(F32)
    diff = ii - jj
    causal = diff >= 0.0
    decay = jnp.where(causal, jnp.exp(lg * jnp.where(causal, diff, 0.0)), 0.0)
    q_decay = jnp.exp(lg * (ii + 1.0))
    k_decay = jnp.exp(lg * (c - 1.0 - ii))
    chunk_decay = jnp.exp(jnp.full((1, 1), lg * c, F32))
    half = RET_DK // 2
    k_scale = RET_DK ** -0.5
    gn_w = gn_ref[...]
    op = (lambda t: t.astype(BF16)) if c % BF16_ROWS == 0 else (lambda t: t)
    outs = []

    for s in range(n_seq):
        state = jnp.concatenate([s0a_ref[s], s0b_ref[s]], axis=0) if has_s0 else s_ref[s]
        for ci in range(chunks_per_seq):
            r0 = (s * chunks_per_seq + ci) * c
            rows = slice(r0, r0 + c)
            cos = cos_ref[rows, :]
            sin = sin_ref[rows, :]
            q = q_ref[rows, :]
            k = k_ref[rows, :]
            q1, q2 = q[:, :half], q[:, half:]
            k1, k2 = k[:, :half], k[:, half:]
            qr = jnp.concatenate([q1 * cos - q2 * sin, q1 * sin + q2 * cos], axis=-1)
            kr = jnp.concatenate([k1 * cos - k2 * sin, k1 * sin + k2 * cos], axis=-1) * k_scale
            vb = op(v_ref[rows, :])
            qb = op(qr)
            scores = _dot_nt(qb, op(kr)) * decay
            inner = _dot(op(scores), vb)
            cross = _dot(qb, op(state)) * q_decay
            o = inner + cross
            state = state * chunk_decay + _dot_tn(op(kr * k_decay), vb)
            mu = jnp.mean(o, axis=-1, keepdims=True)
            oc = o - mu
            var = jnp.mean(oc * oc, axis=-1, keepdims=True)
            on = oc * lax.rsqrt(var + EPS) * gn_w
            res = on * _silu(g_ref[rows, :])
            if c % BF16_ROWS == 0:
                o_ref[rows, :] = res.astype(o_ref.dtype)
            else:
                outs.append(res)
        s_ref[s] = state
    if outs:
        o_ref[...] = jnp.concatenate(outs, axis=0).astype(o_ref.dtype)


def _retention(proj, row0, n_seq_total, seq_len, pos0, s0, gn_w, *, chunk, tb, bb):
    hq = RET_HEADS * RET_DK
    n_rows = n_seq_total * seq_len
    blk_rows = bb * tb
    n_inner = seq_len // tb
    n_outer = n_seq_total // bb
    assert row0 % blk_rows == 0 and seq_len % tb == 0 and tb % chunk == 0
    assert s0 is None or n_inner == 1
    rb0 = row0 // blk_rows
    half = RET_DK // 2

    pos = (pos0 + jnp.arange(seq_len)).astype(F32)
    inv = ROPE_BASE ** (-jnp.arange(half, dtype=F32) / half)
    ang = pos[:, None] * inv[None, :]
    cos = jnp.tile(jnp.cos(ang), (bb, 1)) if n_inner == 1 else jnp.cos(ang)
    sin = jnp.tile(jnp.sin(ang), (bb, 1)) if n_inner == 1 else jnp.sin(ang)
    log_gamma = jnp.log1p(-jnp.exp2(-5.0 - jnp.arange(RET_HEADS, dtype=F32)))

    def row_map(b, h, n):
        return rb0 + b * n_inner + n

    in_specs = [
        pl.BlockSpec(memory_space=pltpu.SMEM),
        pl.BlockSpec((blk_rows, RET_DK), lambda b, h, n: (row_map(b, h, n), h)),
        pl.BlockSpec((blk_rows, RET_DK), lambda b, h, n: (row_map(b, h, n), RET_HEADS + h)),
        pl.BlockSpec((blk_rows, RET_DV), lambda b, h, n: (row_map(b, h, n), 2 * hq // RET_DV + h)),
        pl.BlockSpec((blk_rows, RET_DV), lambda b, h, n: (row_map(b, h, n), 2 * hq // RET_DV + RET_HEADS + h)),
        pl.BlockSpec((blk_rows, half), lambda b, h, n: (n, 0)),
        pl.BlockSpec((blk_rows, half), lambda b, h, n: (n, 0)),
        pl.BlockSpec((None, 1, RET_DV), lambda b, h, n: (h, 0, 0)),
    ]
    args = [log_gamma, proj, proj, proj, proj, cos, sin, gn_w.reshape(RET_HEADS, 1, RET_DV)]
    if s0 is not None:
        in_specs += [pl.BlockSpec((bb, None, RET_DK // 2, RET_DV), lambda b, h, n: (b, h, 0, 0)),
                     pl.BlockSpec((bb, None, RET_DK // 2, RET_DV), lambda b, h, n: (b, h, 1, 0))]
        args += [s0, s0]
    out, s_fin = pl.pallas_call(
        functools.partial(_ret_kernel, chunk=chunk, n_seq=bb, chunks_per_seq=tb // chunk,
                          has_s0=s0 is not None),
        grid=(n_outer, RET_HEADS, n_inner),
        in_specs=in_specs,
        out_specs=[pl.BlockSpec((blk_rows, RET_DV), lambda b, h, n: (b * n_inner + n, h)),
                   pl.BlockSpec((bb, None, RET_DK, RET_DV), lambda b, h, n: (b, h, 0, 0))],
        out_shape=[jax.ShapeDtypeStruct((n_rows, RET_HEADS * RET_DV), BF16),
                   jax.ShapeDtypeStruct((n_seq_total, RET_HEADS, RET_DK, RET_DV), F32)],
        compiler_params=_params(("parallel", "parallel", "arbitrary")),
        name="retention",
    )(*args)
    return out, s_fin


def _shift_rows(x, prev8, s, x_ref=None):
    n = x.shape[0]
    first = x[:SUBLANES]
    row = lax.broadcasted_iota(jnp.int32, first.shape, 0)
    top = jnp.where(row < s, pltpu.roll(prev8, s, 0), pltpu.roll(first, s, 0))
    if n == SUBLANES:
        return top
    if x_ref is None:
        rest = pltpu.roll(x, s, 0)[SUBLANES:]
    else:
        rest = x_ref[pl.ds(SUBLANES - s, n - SUBLANES), :]
    return jnp.concatenate([top, rest], axis=0)


def _causal_conv_silu(x, prev8, cw, x_ref=None):
    acc = x * cw[CONV_W - 1:CONV_W, :]
    for s in range(1, CONV_W):
        acc = acc + _shift_rows(x, prev8, s, x_ref) * cw[CONV_W - 1 - s:CONV_W - s, :]
    return _silu(acc)


def _gdn_kernel(q_ref, k_ref, v_ref, z_ref, gb_ref, alog_ref, dtb_ref, cq_ref, ck_ref, cv_ref,
                nw_ref, *rest, chunk, group, n_seq, rows_per_seq, carry):
    has_init = not carry
    if has_init:
        pq_ref, pk_ref, pv_ref, s0_ref, o_ref, s_ref, prev_ref = rest
    else:
        o_ref, s_ref, prev_ref = rest
    c = chunk
    r = n_seq * rows_per_seq
    m = group
    dk = GDN_DK
    op = (lambda t: t.astype(BF16)) if c % BF16_ROWS == 0 else (lambda t: t)

    if carry:
        @pl.when(pl.program_id(2) == 0)
        def _():
            s_ref[...] = jnp.zeros(s_ref.shape, F32)
            prev_ref[...] = jnp.zeros(prev_ref.shape, F32)

    q_raw, k_raw, v_raw = q_ref[...], k_ref[...], v_ref[...]
    if carry:
        prev = prev_ref[...]
        qc = _causal_conv_silu(q_raw, prev[:, :dk], cq_ref[...], q_ref)
        kc = _causal_conv_silu(k_raw, prev[:, dk:2 * dk], ck_ref[...], k_ref)
        vc = _causal_conv_silu(v_raw, prev[:, 2 * dk:], cv_ref[...], v_ref)
        prev_ref[...] = jnp.concatenate(
            [q_raw[r - SUBLANES:], k_raw[r - SUBLANES:], v_raw[r - SUBLANES:]], axis=-1)
    else:
        qs, ks, vs = [], [], []
        for s in range(n_seq):
            rows = slice(s * rows_per_seq, (s + 1) * rows_per_seq)
            qs.append(_causal_conv_silu(q_raw[rows], pq_ref[s], cq_ref[...]))
            ks.append(_causal_conv_silu(k_raw[rows], pk_ref[s], ck_ref[...]))
            vs.append(_causal_conv_silu(v_raw[rows], pv_ref[s], cv_ref[...]))
        qc = jnp.concatenate(qs, axis=0)
        kc = jnp.concatenate(ks, axis=0)
        vc = jnp.concatenate(vs, axis=0)

    qn = qc * lax.rsqrt(jnp.sum(qc * qc, axis=-1, keepdims=True) + EPS) * (dk ** -0.5)
    kn = kc * lax.rsqrt(jnp.sum(kc * kc, axis=-1, keepdims=True) + EPS)
    qb = qn.astype(BF16)
    kb = kn.astype(BF16)

    gb = gb_ref[...]
    a_coef = jnp.exp(alog_ref[...][:, :1])
    xs = gb + dtb_ref[...][:, :1]
    softplus = jnp.maximum(xs, 0.0) + jnp.log(1.0 + jnp.exp(-jnp.abs(xs)))
    g_rows = -a_coef * softplus
    beta_rows = 1.0 / (1.0 + jnp.exp(-gb))
    rid = lax.broadcasted_iota(jnp.int32, gb.shape, 0)
    x8 = jnp.where(rid < 2, g_rows, beta_rows)

    ri = lax.broadcasted_iota(jnp.int32, (m, m), 0)
    ci = lax.broadcasted_iota(jnp.int32, (m, m), 1)
    log2c = int(math.log2(c))
    same = lax.shift_right_logical(ri, log2c) == lax.shift_right_logical(ci, log2c)
    tri_incl = jnp.logical_and(same, ri >= ci)
    tri_strict = jnp.logical_and(same, ri > ci)
    tri_b = jnp.where(tri_incl, 1.0, 0.0).astype(BF16)
    eye_b = jnp.where(ri == ci, 1.0, 0.0).astype(BF16)
    eye_f = jnp.where(ri == ci, 1.0, 0.0)
    n_sq = max(log2c - 1, 0)
    nw = nw_ref[...]

    units = [(g, vh) for g in range(r // m) for vh in range(2)]
    rows_of = {g: slice(g * m, (g + 1) * m) for g in range(r // m)}
    gate = {}
    for g in range(r // m):
        rows_g = rows_of[g]
        parts = _split3(x8[:, rows_g])
        cs_col = sum(_dot_nt(tri_b, p) for p in parts)
        cs_row = sum(_dot_nt(p, tri_b) for p in parts)
        x_col = sum(_dot_nt(eye_b, p) for p in parts)
        gate[g] = (cs_col, cs_row, x_col, jnp.exp(cs_col),
                   _dot_nt(kb[rows_g], kb[rows_g]), _dot_nt(qb[rows_g], kb[rows_g]))

    g_col, b_col, eg_col, dmat, t_mat, pw = {}, {}, {}, {}, {}, {}
    for u in units:
        g, vh = u
        cs_col, cs_row, x_col, exp_col, kk, _ = gate[g]
        g_col[u] = cs_col[:, vh:vh + 1]
        b_col[u] = x_col[:, 2 + vh:3 + vh]
        eg_col[u] = exp_col[:, vh:vh + 1]
        dmat[u] = jnp.exp(jnp.where(tri_incl, g_col[u] - cs_row[vh:vh + 1, :], -jnp.inf))
        pw[u] = jnp.where(tri_strict, -(kk * dmat[u]) * b_col[u], 0.0)
        t_mat[u] = eye_f + pw[u]
    for _ in range(n_sq):
        for u in units:
            pwb = pw[u].astype(BF16)
            pw[u] = _dot(pwb, pwb)
        for u in units:
            t_mat[u] = t_mat[u] + _dot(t_mat[u].astype(BF16), pw[u].astype(BF16))

    sol, q_in = {}, {}
    for u in units:
        g, vh = u
        v_h = vc[rows_of[g], vh * GDN_DV:(vh + 1) * GDN_DV]
        rhs = jnp.concatenate([v_h * b_col[u], kn[rows_of[g]] * (b_col[u] * eg_col[u])], axis=-1)
        sol[u] = _dot(t_mat[u].astype(BF16), rhs.astype(BF16))
        q_in[u] = qn[rows_of[g]] * eg_col[u]

    chunks = [(u, ch) for u in units for ch in range(m // c)]
    k_dec, g_last, qp = {}, {}, {}
    for uc in chunks:
        u, ch = uc
        rows = slice(ch * c, (ch + 1) * c)
        g_last[uc] = g_col[u][(ch + 1) * c - 1:(ch + 1) * c, :]
        k_dec[uc] = kn[rows_of[u[0]]][rows] * jnp.exp(g_last[uc] - g_col[u][rows])
        if carry:
            qp[uc] = _dot_tn(k_dec[uc].astype(BF16), sol[u][rows].astype(BF16))

    v_new, o_inter = {}, {}
    if carry:
        states = [s_ref[0, vh] for vh in range(2)]
        sb = {}
        for g in range(r // m):
            for ch in range(m // c):
                for vh in range(2):
                    uc = ((g, vh), ch)
                    sb[uc] = states[vh].astype(BF16)
                    states[vh] = (states[vh] * jnp.exp(g_last[uc])
                                  - _dot(qp[uc][:, GDN_DV:].astype(BF16), sb[uc]) + qp[uc][:, :GDN_DV])
        for vh in range(2):
            s_ref[0, vh] = states[vh]
        for uc in chunks:
            u, ch = uc
            rows = slice(ch * c, (ch + 1) * c)
            both = _dot(jnp.concatenate([sol[u][rows, GDN_DV:].astype(BF16),
                                         q_in[u][rows].astype(BF16)], axis=0), sb[uc])
            v_new[uc] = sol[u][rows, :GDN_DV] - both[:c]
            o_inter[uc] = both[c:]
    else:
        for uc in chunks:
            u, ch = uc
            rows = slice(ch * c, (ch + 1) * c)
            seq = (u[0] * m + ch * c) // rows_per_seq
            state = s0_ref[seq, u[1]]
            both = _dot(jnp.concatenate([sol[u][rows, GDN_DV:], q_in[u][rows]], axis=0).astype(BF16),
                        state.astype(BF16))
            v_new[uc] = sol[u][rows, :GDN_DV] - both[:c]
            o_inter[uc] = both[c:]
            s_ref[seq, u[1]] = (state * jnp.exp(g_last[uc])
                                + _dot_tn(op(k_dec[uc]), op(v_new[uc])))

    for u in units:
        g, vh = u
        v_new_all = jnp.concatenate([v_new[(u, ch)] for ch in range(m // c)], axis=0)
        o = jnp.concatenate([o_inter[(u, ch)] for ch in range(m // c)], axis=0)
        attn = gate[g][5] * dmat[u]
        o = o + _dot(attn.astype(BF16), v_new_all.astype(BF16))
        o = o * lax.rsqrt(jnp.mean(o * o, axis=-1, keepdims=True) + EPS) * nw
        z = z_ref[rows_of[g], vh * GDN_DV:(vh + 1) * GDN_DV]
        o_ref[rows_of[g], vh * GDN_DV:(vh + 1) * GDN_DV] = (o * _silu(z)).astype(o_ref.dtype)


def _gdn(proj, gates, row0, n_seq_total, seq_len, conv0_pad, s0, conv_w, a_log, dt_bias, norm_w,
         *, chunk, tb, bb):
    hk = GDN_K_HEADS * GDN_DK
    hv = GDN_V_HEADS * GDN_DV
    n_rows = n_seq_total * seq_len
    blk_rows = bb * tb
    n_inner = seq_len // tb
    n_outer = n_seq_total // bb
    carry = s0 is None
    assert row0 % blk_rows == 0 and seq_len % tb == 0 and tb % chunk == 0
    assert carry == (bb == 1) and (carry or (n_inner == 1 and tb == chunk))
    assert blk_rows % GDN_GROUP == 0 and GDN_GROUP % chunk == 0
    rb0 = row0 // blk_rows
    two = 2 * GDN_DV

    def row_map(b, h, n):
        return rb0 + b * n_inner + n

    alog8 = jnp.zeros((GDN_K_HEADS, SUBLANES, LANES), F32).at[:, :2, :].set(
        jnp.broadcast_to(a_log.reshape(GDN_K_HEADS, 2, 1), (GDN_K_HEADS, 2, LANES)))
    dtb8 = jnp.zeros((GDN_K_HEADS, SUBLANES, LANES), F32).at[:, :2, :].set(
        jnp.broadcast_to(dt_bias.reshape(GDN_K_HEADS, 2, 1), (GDN_K_HEADS, 2, LANES)))

    in_specs = [
        pl.BlockSpec((blk_rows, GDN_DK), lambda b, h, n: (row_map(b, h, n), h)),
        pl.BlockSpec((blk_rows, GDN_DK), lambda b, h, n: (row_map(b, h, n), GDN_K_HEADS + h)),
        pl.BlockSpec((blk_rows, two), lambda b, h, n: (row_map(b, h, n), 2 * hk // two + h)),
        pl.BlockSpec((blk_rows, two), lambda b, h, n: (row_map(b, h, n), CONV_DIM // two + h)),
        pl.BlockSpec((None, SUBLANES, blk_rows), lambda b, h, n: (h, 0, row_map(b, h, n))),
        pl.BlockSpec((None, SUBLANES, LANES), lambda b, h, n: (h, 0, 0)),
        pl.BlockSpec((None, SUBLANES, LANES), lambda b, h, n: (h, 0, 0)),
        pl.BlockSpec((CONV_W, GDN_DK), lambda b, h, n: (0, h)),
        pl.BlockSpec((CONV_W, GDN_DK), lambda b, h, n: (0, GDN_K_HEADS + h)),
        pl.BlockSpec((CONV_W, two), lambda b, h, n: (0, 2 * hk // two + h)),
        pl.BlockSpec((1, GDN_DV), lambda b, h, n: (0, 0)),
    ]
    args = [proj, proj, proj, proj, gates, alog8, dtb8, conv_w, conv_w, conv_w,
            norm_w.reshape(1, GDN_DV)]
    if not carry:
        in_specs += [
            pl.BlockSpec((bb, SUBLANES, GDN_DK), lambda b, h, n: (b, 0, h)),
            pl.BlockSpec((bb, SUBLANES, GDN_DK), lambda b, h, n: (b, 0, GDN_K_HEADS + h)),
            pl.BlockSpec((bb, SUBLANES, two), lambda b, h, n: (b, 0, 2 * hk // two + h)),
            pl.BlockSpec((bb, 2, GDN_DK, GDN_DV), lambda b, h, n: (b, h, 0, 0)),
        ]
        args += [conv0_pad, conv0_pad, conv0_pad, s0]
    out, s_fin = pl.pallas_call(
        functools.partial(_gdn_kernel, chunk=chunk, group=GDN_GROUP, n_seq=bb, rows_per_seq=tb,
                          carry=carry),
        grid=(n_outer, GDN_K_HEADS, n_inner),
        in_specs=in_specs,
        out_specs=[pl.BlockSpec((blk_rows, two), lambda b, h, n: (b * n_inner + n, h)),
                   pl.BlockSpec((bb, 2, GDN_DK, GDN_DV), lambda b, h, n: (b, h, 0, 0))],
        out_shape=[jax.ShapeDtypeStruct((n_rows, hv), BF16),
                   jax.ShapeDtypeStruct((n_seq_total, GDN_V_HEADS, GDN_DK, GDN_DV), F32)],
        scratch_shapes=[pltpu.VMEM((SUBLANES, 2 * GDN_DK + two), F32)],
        compiler_params=_params(("parallel", "parallel", "arbitrary")),
        name="gated_delta",
    )(*args)
    return out, s_fin


def _router_kernel(x_ref, w_ref, wr_ref, br_ref, h_ref, lg_ref):
    x = x_ref[...]
    ms = jnp.mean(x * x, axis=-1, keepdims=True)
    h = x * lax.rsqrt(ms + EPS) * w_ref[...]
    h_ref[...] = h
    lg_ref[...] = _dot_nt(wr_ref[...].astype(BF16), h.astype(BF16)) + br_ref[...]


def _router(x, norm_w, w_route, b_route):
    t, d = x.shape
    return pl.pallas_call(
        _router_kernel,
        grid=(t // ROW_TILE,),
        in_specs=[pl.BlockSpec((ROW_TILE, d), lambda i: (i, 0)),
                  pl.BlockSpec((1, d), lambda i: (0, 0)),
                  pl.BlockSpec((LANES, d), lambda i: (0, 0)),
                  pl.BlockSpec((LANES, 1), lambda i: (0, 0))],
        out_specs=[pl.BlockSpec((ROW_TILE, d), lambda i: (i, 0)),
                   pl.BlockSpec((LANES, ROW_TILE), lambda i: (0, i))],
        out_shape=[jax.ShapeDtypeStruct((t, d), F32),
                   jax.ShapeDtypeStruct((LANES, t), F32)],
        compiler_params=_params(("parallel",)),
        name="moe_router",
    )(x, norm_w.reshape(1, d), w_route, b_route)


def _row_gather_start(idx_ref, base, n, src_hbm, dst_ref, sem):
    def body(g, carry):
        for k in range(GATHER_UNROLL):
            r = g * GATHER_UNROLL + k
            tok = idx_ref[base + r]
            pltpu.make_async_copy(src_hbm.at[pl.ds(tok, 1)], dst_ref.at[pl.ds(r, 1)], sem).start(priority=k % 2)
        return carry
    lax.fori_loop(0, n // GATHER_UNROLL, body, 0)


def _row_gather_wait(n, src_hbm, dst_ref, sem):
    pltpu.make_async_copy(src_hbm.at[pl.ds(0, n)], dst_ref, sem).wait()


def _row_gather_start_counted(idx_ref, base, n, src_hbm, dst_ref, sem):
    def body(g, carry):
        for k in range(GATHER_UNROLL):
            r = g * GATHER_UNROLL + k
            tok = idx_ref[base + r]
            pltpu.make_async_copy(src_hbm.at[pl.ds(tok, 1)], dst_ref.at[pl.ds(r, 1)], sem).start()
        return carry
    lax.fori_loop(0, n // GATHER_UNROLL, body, 0)


def _row_gather_wait_counted(n, src_hbm, dst_ref, sem):
    @pl.when(n > 0)
    def _():
        rows = pl.multiple_of(n, GATHER_UNROLL)
        pltpu.make_async_copy(src_hbm.at[pl.ds(0, rows)], dst_ref.at[pl.ds(0, rows)], sem).wait()


def _moe_ffn_kernel(be_ref, first_ref, nxt_ref, ws_ref, cnt_ref, rt_ref, na_ref, h_hbm, rw_ref,
                    win_hbm, wout_hbm, o_ref, xbuf, sem, win_f, wout_f, win_sem, wout_sem, winb, woutb,
                    *, layer):
    i = pl.program_id(0)
    n_active = na_ref[0]
    slot = lax.rem(i, MOE_GATHER_SLOTS)

    def gather_start(blk):
        s = lax.rem(blk, MOE_GATHER_SLOTS)
        _row_gather_start_counted(rt_ref, blk * MOE_BLOCK, cnt_ref[blk], h_hbm, xbuf.at[s], sem.at[s])

    @pl.when(i == 0)
    def _():
        xbuf[...] = jnp.zeros(xbuf.shape, F32)

    def weight_copies(e, ws):
        r_in = win_f.shape[1] // MOE_W_SLABS
        r_out = wout_f.shape[1] // MOE_W_SLABS
        cps = []
        for k in range(MOE_W_SLABS):
            cps.append(pltpu.make_async_copy(win_hbm.at[layer, e, pl.ds(k * r_in, r_in)],
                                             win_f.at[ws, pl.ds(k * r_in, r_in)], win_sem.at[ws, k]))
            cps.append(pltpu.make_async_copy(wout_hbm.at[layer, e, pl.ds(k * r_out, r_out)],
                                             wout_f.at[ws, pl.ds(k * r_out, r_out)], wout_sem.at[ws, k]))
        return cps

    @pl.when(jnp.logical_and(i == 0, n_active > 0))
    def _():
        for cp in weight_copies(be_ref[0], ws_ref[0]):
            cp.start(priority=BULK_DMA_PRIORITY)
        for blk in range(MOE_GATHER_SLOTS - 1):
            pl.when(blk < n_active)(functools.partial(gather_start, blk))

    @pl.when(i + MOE_GATHER_SLOTS - 1 < n_active)
    def _():
        gather_start(i + MOE_GATHER_SLOTS - 1)

    @pl.when(i < n_active)
    def _():
        ws = ws_ref[i]

        @pl.when(first_ref[i] == 1)
        def _():
            @pl.when(nxt_ref[i] >= 0)
            def _():
                for cp in weight_copies(nxt_ref[i], 1 - ws):
                    cp.start(priority=BULK_DMA_PRIORITY)

            for cp in weight_copies(be_ref[i], ws):
                cp.wait()
            winb[...] = win_f[ws].astype(BF16)
            woutb[...] = wout_f[ws].astype(BF16)

        _row_gather_wait_counted(cnt_ref[i], h_hbm, xbuf.at[slot], sem.at[slot])
        x = xbuf[slot].astype(BF16)
        mid = _dot(x, winb[...])
        act = (_silu(mid[:, :D_EXPERT]) * mid[:, D_EXPERT:]).astype(BF16)
        o_ref[...] = _dot(act, woutb[...]) * rw_ref[...]

    @pl.when(i >= n_active)
    def _():
        o_ref[...] = jnp.zeros(o_ref.shape, F32)


def _moe_ffn(h, block_e, block_first, block_next, block_wslot, block_cnt, row_tok, n_active, row_w,
             w_in, w_out, layer):
    t, d = h.shape
    n_blocks = block_e.shape[0]
    n_rows = n_blocks * MOE_BLOCK
    grid_spec = pltpu.PrefetchScalarGridSpec(
        num_scalar_prefetch=7,
        grid=(n_blocks,),
        in_specs=[
            pl.BlockSpec(memory_space=pl.ANY),
            pl.BlockSpec((MOE_BLOCK, 1), lambda i, *_: (i, 0)),
            pl.BlockSpec(memory_space=pl.ANY),
            pl.BlockSpec(memory_space=pl.ANY),
        ],
        out_specs=pl.BlockSpec((MOE_BLOCK, d), lambda i, *_: (i, 0)),
        scratch_shapes=[pltpu.VMEM((MOE_GATHER_SLOTS, MOE_BLOCK, d), F32),
                        pltpu.SemaphoreType.DMA((MOE_GATHER_SLOTS,)),
                        pltpu.VMEM((2, d, 2 * D_EXPERT), F32),
                        pltpu.VMEM((2, D_EXPERT, d), F32),
                        pltpu.SemaphoreType.DMA((2, MOE_W_SLABS)),
                        pltpu.SemaphoreType.DMA((2, MOE_W_SLABS)),
                        pltpu.VMEM((d, 2 * D_EXPERT), BF16),
                        pltpu.VMEM((D_EXPERT, d), BF16)],
    )
    return pl.pallas_call(
        functools.partial(_moe_ffn_kernel, layer=layer),
        grid_spec=grid_spec,
        out_shape=jax.ShapeDtypeStruct((n_rows, d), F32),
        compiler_params=_params(("arbitrary",)),
        name="moe_ffn",
    )(block_e, block_first, block_next, block_wslot, block_cnt, row_tok, n_active, h,
      row_w.reshape(n_rows, 1), w_in, w_out)


def _combine_kernel(pos_ref, ys_hbm, x_ref, *rest, with_norm, final_split):
    if final_split is not None:
        nw_ref, y1_ref, y2_ref, buf, sem = rest
    elif with_norm:
        nw_ref, o_ref, hn_ref, buf, sem = rest
    else:
        o_ref, buf, sem = rest
    i = pl.program_id(0)
    n = pl.num_programs(0)
    slot = lax.rem(i, COMBINE_SLOTS)
    rows = 2 * COMBINE_TM

    def gather_start(tile):
        s = lax.rem(tile, COMBINE_SLOTS)
        _row_gather_start(pos_ref, tile * rows, rows, ys_hbm, buf.at[s], sem.at[s])

    @pl.when(i == 0)
    def _():
        for tile in range(COMBINE_SLOTS - 1):
            pl.when(tile < n)(functools.partial(gather_start, tile))

    @pl.when(i + COMBINE_SLOTS - 1 < n)
    def _():
        gather_start(i + COMBINE_SLOTS - 1)

    _row_gather_wait(rows, ys_hbm, buf.at[slot], sem.at[slot])
    y = x_ref[...] + buf[slot, :COMBINE_TM, :] + buf[slot, COMBINE_TM:, :]
    if final_split is None:
        o_ref[...] = y
    if with_norm or final_split is not None:
        ms = jnp.mean(y * y, axis=-1, keepdims=True)
        yn = y * lax.rsqrt(ms + EPS) * nw_ref[...]
    if with_norm:
        hn_ref[...] = yn.astype(hn_ref.dtype)
    if final_split is not None:
        @pl.when(i < final_split)
        def _():
            y1_ref[...] = yn

        @pl.when(i >= final_split)
        def _():
            y2_ref[...] = yn


def _combine(x, ys, pos, next_norm_w=None, final=None):
    t, d = x.shape
    with_norm = next_norm_w is not None
    row_spec = pl.BlockSpec((COMBINE_TM, d), lambda i, p: (i, 0))
    in_specs = [pl.BlockSpec(memory_space=pl.ANY), row_spec]
    args = [pos, ys, x]
    out_specs, out_shape = row_spec, jax.ShapeDtypeStruct((t, d), F32)
    final_split = None
    if final is not None:
        norm_w, n_first = final
        assert not with_norm and n_first % COMBINE_TM == 0 and 0 < n_first < t
        final_split = n_first // COMBINE_TM
        in_specs.append(pl.BlockSpec((1, d), lambda i, p: (0, 0)))
        args.append(norm_w.reshape(1, d))
        out_specs = [pl.BlockSpec((COMBINE_TM, d), lambda i, p: (jnp.minimum(i, final_split - 1), 0)),
                     pl.BlockSpec((COMBINE_TM, d), lambda i, p: (jnp.maximum(i - final_split, 0), 0))]
        out_shape = [jax.ShapeDtypeStruct((n_first, d), F32), jax.ShapeDtypeStruct((t - n_first, d), F32)]
    if with_norm:
        in_specs.append(pl.BlockSpec((1, d), lambda i, p: (0, 0)))
        args.append(next_norm_w.reshape(1, d))
        out_specs = [row_spec, row_spec]
        out_shape = [out_shape, jax.ShapeDtypeStruct((t, d), BF16)]
    grid_spec = pltpu.PrefetchScalarGridSpec(
        num_scalar_prefetch=1,
        grid=(t // COMBINE_TM,),
        in_specs=in_specs,
        out_specs=out_specs,
        scratch_shapes=[pltpu.VMEM((COMBINE_SLOTS, 2 * COMBINE_TM, d), F32),
                        pltpu.SemaphoreType.DMA((COMBINE_SLOTS,))],
    )
    return pl.pallas_call(
        functools.partial(_combine_kernel, with_norm=with_norm, final_split=final_split),
        grid_spec=grid_spec,
        out_shape=out_shape,
        compiler_params=_params(("arbitrary",)),
        name="moe_combine",
    )(*args)


def _hier_moe(x, norm_w, w_group, b_group, w_expert, b_expert, w_in, w_out, layer, next_norm_w=None,
              final=None):
    t, d = x.shape
    n_route = MOE_GROUPS + N_EXPERTS
    w_route = jnp.zeros((LANES, d), F32).at[:MOE_GROUPS].set(w_group.T).at[MOE_GROUPS:n_route].set(w_expert.T)
    b_route = jnp.zeros((LANES, 1), F32).at[:MOE_GROUPS, 0].set(b_group).at[MOE_GROUPS:n_route, 0].set(b_expert)
    h, logits = _router(x, norm_w, w_route, b_route)

    g_logits = logits[:MOE_GROUPS]
    g_max = jnp.max(g_logits, axis=0)
    g_sel = jnp.argmax(g_logits, axis=0).astype(jnp.int32)
    g_w = 1.0 / jnp.sum(jnp.exp(g_logits - g_max[None, :]), axis=0)
    e_all = logits[MOE_GROUPS:n_route].reshape(MOE_GROUPS, EXPERTS_PER_GROUP, t)
    grp = jnp.arange(MOE_GROUPS, dtype=jnp.int32)[:, None, None]
    e_logits = jnp.sum(jnp.where(grp == g_sel[None, None, :], e_all, 0.0), axis=0)
    slot_id = jnp.arange(EXPERTS_PER_GROUP, dtype=jnp.int32)[:, None]
    i1 = jnp.argmax(e_logits, axis=0).astype(jnp.int32)
    v1 = jnp.max(e_logits, axis=0)
    rest = jnp.where(slot_id == i1[None, :], -jnp.inf, e_logits)
    i2 = jnp.argmax(rest, axis=0).astype(jnp.int32)
    v2 = jnp.max(rest, axis=0)
    e21 = jnp.exp(v2 - v1)
    weights = jnp.stack([g_w / (1.0 + e21), g_w * e21 / (1.0 + e21)], axis=1)
    expert_id = jnp.stack([g_sel * EXPERTS_PER_GROUP + i1, g_sel * EXPERTS_PER_GROUP + i2], axis=1)

    n_assign = t * TOP_K
    flat_e = expert_id.reshape(n_assign)
    iota = jnp.arange(n_assign, dtype=jnp.int32)
    experts = jnp.arange(N_EXPERTS, dtype=jnp.int32)
    sorted_e, order, w_sorted = lax.sort((flat_e, iota, weights.reshape(n_assign)), num_keys=1,
                                         is_stable=True)
    counts = jnp.sum((flat_e[:, None] == experts[None, :]).astype(jnp.int32), axis=0)
    start = jnp.cumsum(counts) - counts
    nblk_e = (counts + MOE_BLOCK - 1) // MOE_BLOCK
    blk_end = jnp.cumsum(nblk_e)
    blk_start = blk_end - nblk_e

    def lookup(idx, table):
        return jnp.sum(jnp.where(idx[:, None] == experts[None, :], table[None, :], 0), axis=1)

    dest_sorted = iota + lookup(sorted_e, blk_start * MOE_BLOCK - start)
    _, dest = lax.sort((order, dest_sorted), num_keys=1)

    n_blocks = -(-n_assign // MOE_BLOCK) + N_EXPERTS
    n_active = blk_end[-1].astype(jnp.int32)
    blk = jnp.minimum(jnp.arange(n_blocks, dtype=jnp.int32), jnp.maximum(n_active - 1, 0))
    block_e = jnp.minimum(jnp.sum((blk[:, None] >= blk_end[None, :]).astype(jnp.int32), axis=1),
                          N_EXPERTS - 1)
    block_first = (blk == lookup(block_e, blk_start)).astype(jnp.int32)
    active = counts > 0
    ordinal = jnp.cumsum(active.astype(jnp.int32)) - 1
    block_wslot = lookup(block_e, ordinal) % 2
    later = lax.cummin(jnp.where(active, experts, N_EXPERTS), reverse=True)
    next_e = jnp.concatenate([later[1:], jnp.full((1,), N_EXPERTS, jnp.int32)])
    block_next = lookup(block_e, jnp.where(next_e >= N_EXPERTS, -1, next_e))

    j_base = lookup(block_e, start) + (blk - lookup(block_e, blk_start)) * MOE_BLOCK
    j_end = lookup(block_e, start + counts)
    j = j_base[:, None] + jnp.arange(MOE_BLOCK, dtype=jnp.int32)[None, :]
    valid = jnp.logical_and(j < j_end[:, None],
                            jnp.arange(n_blocks, dtype=jnp.int32)[:, None] < n_active)
    j = jnp.clip(j, 0, n_assign - 1).reshape(-1)
    valid = valid.reshape(-1)
    row_tok = jnp.where(valid, order[j] // TOP_K, 0)
    row_w = jnp.where(valid, w_sorted[j], 0.0)
    block_cnt = jnp.where(jnp.arange(n_blocks, dtype=jnp.int32) < n_active,
                          jnp.clip(j_end - j_base, 0, MOE_BLOCK), 0)
    block_cnt = ((block_cnt + GATHER_UNROLL - 1) // GATHER_UNROLL) * GATHER_UNROLL

    ys = _moe_ffn(h, block_e, block_first, block_next, block_wslot, block_cnt.astype(jnp.int32), row_tok,
                  n_active.reshape(1), row_w, w_in, w_out, layer)
    pos = dest.reshape(t // COMBINE_TM, COMBINE_TM, TOP_K).transpose(0, 2, 1).reshape(n_assign)
    return _combine(x, ys, pos, next_norm_w, final)


def kernel(x_prompt, x_sample, state_ret, state_gdn, state_conv, norm_mix, norm_ffn, norm_final,
           ret_w_in, ret_gn, ret_w_out, gdn_w_in, gdn_conv, gdn_a_log, gdn_dt_bias, gdn_norm,
           gdn_w_out, moe_w_group, moe_b_group, moe_w_expert, moe_b_expert, moe_w_in, moe_w_out):
    bp, lp, d = x_prompt.shape
    bs, ls, _ = x_sample.shape
    tp, ts = bp * lp, bs * ls
    x = (x_prompt.reshape(tp, d), x_sample.reshape(ts, d))

    h = _rmsnorm(x, norm_mix[0], BF16)
    ret_in = ret_w_in.shape[-1]
    proj = _matmul(h, ret_w_in, ret_in, MM_TM, MM_TN, layer=0)
    o_p, ret_p = _retention(proj, 0, bp, lp, 0, None, ret_gn[0],
                            chunk=RET_CHUNK, tb=RET_TB, bb=1)
    o_s, ret_s = _retention(proj, tp, bs, ls, PAST_LEN, state_ret[0], ret_gn[0],
                            chunk=ls, tb=ls, bb=RET_SAMPLE_BB)
    x = _matmul((o_p, o_s), ret_w_out, d, ROW_TILE, MM_TN_OUT, res=x, layer=0)
    x, h = _hier_moe(x, norm_ffn[0], moe_w_group[0], moe_b_group[0], moe_w_expert[0], moe_b_expert[0],
                     moe_w_in, moe_w_out, 0, next_norm_w=norm_mix[1])

    hv = GDN_V_HEADS * GDN_DV
    n_main = CONV_DIM + hv
    proj = _matmul(h, jnp.transpose(gdn_w_in[0]), n_main, MM_TM, MM_TN, w_is_nk=True)
    w_tail = jnp.zeros((d, LANES), F32).at[:, :2 * GDN_V_HEADS].set(gdn_w_in[0, :, n_main:])
    tail = _matmul(h, w_tail, LANES, MM_TM, LANES)
    t = tp + ts
    bt = tail[:, :GDN_V_HEADS].reshape(t, GDN_K_HEADS, 2)
    a = tail[:, GDN_V_HEADS:2 * GDN_V_HEADS].reshape(t, GDN_K_HEADS, 2)
    gates = jnp.concatenate([a, bt, jnp.zeros((t, GDN_K_HEADS, 4), F32)], axis=-1).transpose(1, 2, 0)
    conv0_pad = jnp.pad(state_conv[0], ((0, 0), (SUBLANES - (CONV_W - 1), 0), (0, 0)))
    o_p, gdn_p = _gdn(proj, gates, 0, bp, lp, None, None, gdn_conv[0], gdn_a_log[0], gdn_dt_bias[0],
                      gdn_norm[0], chunk=GDN_CHUNK, tb=GDN_TB, bb=1)
    o_s, gdn_s = _gdn(proj, gates, tp, bs, ls, conv0_pad, state_gdn[0], gdn_conv[0], gdn_a_log[0],
                      gdn_dt_bias[0], gdn_norm[0], chunk=ls, tb=ls, bb=GDN_SAMPLE_BB)
    proj8 = proj.reshape(t // SUBLANES, SUBLANES, n_main)
    keep = SUBLANES - (CONV_W - 1)
    conv_p = jnp.concatenate(
        [lax.slice(proj8, ((b * lp + lp) // SUBLANES - 1, keep, 0),
                   ((b * lp + lp) // SUBLANES, SUBLANES, CONV_DIM)) for b in range(bp)], axis=0)
    conv_s = lax.slice(proj8, ((tp + ls) // SUBLANES - 1, keep, 0), (t // SUBLANES, SUBLANES, CONV_DIM),
                       (ls // SUBLANES, 1, 1))
    x = _matmul((o_p, o_s), gdn_w_out, d, ROW_TILE, MM_TN_OUT, res=x, layer=0)
    y_p, y_s = _hier_moe(x, norm_ffn[1], moe_w_group[1], moe_b_group[1], moe_w_expert[1], moe_b_expert[1],
                         moe_w_in, moe_w_out, 1, final=(norm_final, tp))
    return (y_p.reshape(bp, lp, d), y_s.reshape(bs, ls, d),
            ret_p[None], ret_s[None], gdn_p[None], gdn_s[None], conv_p[None], conv_s[None])
```
